```python
import math
import jax
import jax.numpy as jnp
from jax import lax
import numpy as np

D_MODEL = 2048
BATCH = 4
SEQ = 2048
DEPTH = 4
DEC_BATCH = 32
DEC_SEQ = 1
PAST_LEN = 16384
PAGE_SIZE = 128

N_MIXERS = 4
N_SSD = (DEPTH + 3) // N_MIXERS
N_SWA = (DEPTH + 2) // N_MIXERS
N_GDN = (DEPTH + 1) // N_MIXERS
N_NSA = DEPTH // N_MIXERS
EPS = 1e-6
D_FF = 5504

SSD_D_INNER = 2 * D_MODEL
SSD_HEADDIM = 64
SSD_HEADS = SSD_D_INNER // SSD_HEADDIM
SSD_GROUPS = 8
SSD_STATE = 128
SSD_CONV = 4
SSD_CONV_DIM = SSD_D_INNER + 2 * SSD_GROUPS * SSD_STATE
SSD_CHUNK = 128

HEAD_DIM = 128
ATT_HEADS = D_MODEL // HEAD_DIM
REL_BUCKETS = 32
REL_MAX_DIST = 128
ATT_BLOCK = 128

SWA_KV = 4
SWA_WINDOW = 128

GDN_K_HEADS = D_MODEL // 128
GDN_V_HEADS = 2 * GDN_K_HEADS
GDN_DK = 128
GDN_DV = 128
GDN_KEY_DIM = GDN_K_HEADS * GDN_DK
GDN_VAL_DIM = GDN_V_HEADS * GDN_DV
GDN_CONV = 4
GDN_CONV_DIM = 2 * GDN_KEY_DIM + GDN_VAL_DIM
GDN_CHUNK = 64

NSA_KV = 2
NSA_BLOCK = 64
NSA_TOPN = 16
NSA_WINDOW = 512
NSA_CMP_HID = 256
NSA_FORCE = 1e4

kernel_name = 'hybrid_ssd_swa_gdn_nsa_macaron_step'


def rmsnorm(x, g):
    xf = x.astype(jnp.float32)
    y = xf * lax.rsqrt(jnp.mean(xf * xf, axis=-1, keepdims=True) + EPS)
    return (y * g.astype(jnp.float32)).astype(x.dtype)


def l2norm(x):
    xf = x.astype(jnp.float32)
    return xf * lax.rsqrt(jnp.sum(xf * xf, axis=-1, keepdims=True) + EPS)


def swiglu(x, w_gate, w_up, w_down):
    return (jax.nn.silu(x @ w_gate) * (x @ w_up)) @ w_down


def causal_conv(x, prev, w, b=None):
    width, T = w.shape[0], x.shape[1]
    xp = jnp.concatenate([prev.astype(x.dtype), x], axis=1)
    out = xp[:, width - 1:width - 1 + T] * w[width - 1]
    for j in range(width - 1):
        out = out + xp[:, j:j + T] * w[j]
    if b is not None:
        out = out + b
    return out, xp[:, T:]


def rel_bucket(dist):
    exact = REL_BUCKETS // 2
    d = jnp.maximum(dist, 0)
    logd = jnp.log(jnp.maximum(d, 1).astype(jnp.float32) / exact)
    far = exact + (logd / math.log(REL_MAX_DIST / exact) * (REL_BUCKETS - exact)).astype(jnp.int32)
    return jnp.where(d < exact, d, jnp.minimum(far, REL_BUCKETS - 1))


def masked_softmax(logits, mask):
    logits = jnp.where(mask, logits, -1e30)
    m = jnp.max(logits, axis=-1, keepdims=True)
    p = jnp.where(mask, jnp.exp(logits - m), 0.0)
    return p / jnp.maximum(jnp.sum(p, axis=-1, keepdims=True), 1e-30)


def softmax_with_sink(logits, sink, mask):
    logits = jnp.where(mask, logits, -jnp.inf)
    m = jnp.maximum(jnp.max(logits, axis=-1, keepdims=True), sink)
    p = jnp.exp(logits - m)
    return p / (jnp.sum(p, axis=-1, keepdims=True) + jnp.exp(sink - m))


def local_attend(q, k, v, qpos, kpos, window, table, sink):
    b, Q, G, R, hd = q.shape
    K = k.shape[1]
    logits = jnp.einsum('bqgrd,bkgd->bgrqk', q, k).astype(jnp.float32)
    dist = qpos[:, None] - kpos[None, :]
    mask = (dist >= 0) & (dist <= window) & (kpos[None, :] >= 0)
    logits = logits + table.T[:, rel_bucket(dist)].reshape(G, R, Q, K).astype(jnp.float32)
    if sink is None:
        p = masked_softmax(logits, mask)
    else:
        p = softmax_with_sink(logits, sink.reshape(G, R, 1, 1).astype(jnp.float32), mask)
    return jnp.einsum('bgrqk,bkgd->bqgrd', p.astype(v.dtype), v)


def band_blocks(x, n_prev):
    nb = x.shape[1]
    parts = []
    for s in range(n_prev, 0, -1):
        shifted = jnp.concatenate([jnp.zeros_like(x[:, :s]), x[:, :nb - s]], axis=1)[:, :nb]
        parts.append(shifted)
    parts.append(x)
    return jnp.concatenate(parts, axis=2)


def banded_attention(q, k, v, window, table, sink):
    b, T, G, R, hd = q.shape
    nb = T // ATT_BLOCK
    n_prev = -(-window // ATT_BLOCK)
    qb = q.reshape(b, nb, ATT_BLOCK, G, R, hd)
    kb = band_blocks(k.reshape(b, nb, ATT_BLOCK, G, hd), n_prev)
    vb = band_blocks(v.reshape(b, nb, ATT_BLOCK, G, hd), n_prev)
    starts = jnp.arange(nb) * ATT_BLOCK
    qpos = starts[:, None] + jnp.arange(ATT_BLOCK)[None, :]
    kpos = starts[:, None] - n_prev * ATT_BLOCK + jnp.arange((n_prev + 1) * ATT_BLOCK)[None, :]
    attend = lambda qq, kk, vv, qp, kp: local_attend(qq, kk, vv, qp, kp, window, table, sink)
    out = jax.vmap(attend, in_axes=(1, 1, 1, 0, 0), out_axes=1)(qb, kb, vb, qpos, kpos)
    return out.reshape(b, T, G, R, hd)


def ssd_scan(x, dt, a, bm, cm, h0):
    b, T, g, r, p = x.shape
    n = bm.shape[-1]
    L = math.gcd(T, SSD_CHUNK)
    c = T // L
    da = (dt * a).reshape(b, c, L, g, r)
    xdt = (x * dt[..., None]).reshape(b, c, L, g, r, p)
    bc = bm.reshape(b, c, L, g, n)
    cc = cm.reshape(b, c, L, g, n)
    cs = jnp.cumsum(da, axis=2)
    causal = jnp.tril(jnp.ones((L, L), dtype=bool))[:, :, None, None]
    decay = jnp.exp(jnp.where(causal, cs[:, :, :, None] - cs[:, :, None], -jnp.inf))
    cb = jnp.einsum('bclgn,bcsgn->bclsg', cc, bc)
    y_diag = jnp.einsum('bclsgr,bcsgrp->bclgrp', cb[..., None] * decay, xdt)
    to_end = jnp.exp(cs[:, :, -1:] - cs)
    chunk_states = jnp.einsum('bcsgn,bcsgrp->bcgrpn', bc, xdt * to_end[..., None])
    chunk_decay = jnp.exp(cs[:, :, -1])

    def step(hc, inp):
        st, dec = inp
        return hc * dec[..., None, None] + st, hc

    h_final, h_prev = lax.scan(step, h0, (jnp.moveaxis(chunk_states, 1, 0), jnp.moveaxis(chunk_decay, 1, 0)))
    h_prev = jnp.moveaxis(h_prev, 0, 1)
    y_off = jnp.einsum('bclgn,bcgrpn->bclgrp', cc, h_prev) * jnp.exp(cs)[..., None]
    return (y_diag + y_off).reshape(b, T, g, r, p), h_final


def ssd_mixer(h, conv_prev, ssm_prev, w_in, conv_w, conv_b, dt_bias, a_log, d_skip, norm_g, w_out):
    b, T, _ = h.shape
    G, R = SSD_GROUPS, SSD_HEADS // SSD_GROUPS
    f32 = jnp.float32
    z, xbc, dt = jnp.split(h @ w_in, [SSD_D_INNER, SSD_D_INNER + SSD_CONV_DIM], axis=-1)
    xbc, conv_new = causal_conv(xbc, conv_prev, conv_w, conv_b)
    xbc = jax.nn.silu(xbc)
    xs, bm, cm = jnp.split(xbc, [SSD_D_INNER, SSD_D_INNER + SSD_GROUPS * SSD_STATE], axis=-1)
    xs = xs.reshape(b, T, G, R, SSD_HEADDIM).astype(f32)
    bm = bm.reshape(b, T, G, SSD_STATE).astype(f32)
    cm = cm.reshape(b, T, G, SSD_STATE).astype(f32)
    dt = jax.nn.softplus(dt.astype(f32) + dt_bias.astype(f32)).reshape(b, T, G, R)
    a = -jnp.exp(a_log.astype(f32)).reshape(G, R)
    h0 = ssm_prev.astype(f32).reshape(b, G, R, SSD_HEADDIM, SSD_STATE)
    y, s_new = ssd_scan(xs, dt, a, bm, cm, h0)
    y = y + xs * d_skip.astype(f32).reshape(G, R, 1)
    yz = (y.reshape(b, T, SSD_D_INNER) * jax.nn.silu(z.astype(f32))).reshape(b, T, G, SSD_D_INNER // G)
    y = rmsnorm(yz, norm_g.reshape(G, SSD_D_INNER // G)).reshape(b, T, SSD_D_INNER)
    return y.astype(h.dtype) @ w_out, conv_new, s_new.reshape(b, SSD_HEADS, SSD_HEADDIM, SSD_STATE)


def swa_project(h, w_qkv):
    b, T, _ = h.shape
    R = ATT_HEADS // SWA_KV
    q, k, v = jnp.split(h @ w_qkv, [ATT_HEADS * HEAD_DIM, (ATT_HEADS + SWA_KV) * HEAD_DIM], axis=-1)
    q = q.reshape(b, T, SWA_KV, R, HEAD_DIM) * HEAD_DIM ** -0.5
    return q, k.reshape(b, T, SWA_KV, HEAD_DIM), v.reshape(b, T, SWA_KV, HEAD_DIM)


def swa_prompt(h, rel_table, w_qkv, sink, w_o):
    b, T, _ = h.shape
    q, k, v = swa_project(h, w_qkv)
    o = banded_attention(q, k, v, SWA_WINDOW, rel_table, sink)
    buf = min(SWA_WINDOW, T)
    return o.reshape(b, T, -1) @ w_o, k[:, T - buf:], v[:, T - buf:]


def swa_sample(h, k_buf, v_buf, rel_table, w_qkv, sink, w_o):
    b, T, _ = h.shape
    buf = k_buf.shape[1]
    q, k, v = swa_project(h, w_qkv)
    kk = jnp.concatenate([k_buf.astype(k.dtype), k], axis=1)
    vv = jnp.concatenate([v_buf.astype(v.dtype), v], axis=1)
    qpos = PAST_LEN + jnp.arange(T)
    kpos = PAST_LEN - buf + jnp.arange(buf + T)
    o = local_attend(q, kk, vv, qpos, kpos, SWA_WINDOW, rel_table, sink)
    return o.reshape(b, T, -1) @ w_o, kk[:, T:], vv[:, T:]


def gated_delta_chunked(q, k, v, g, beta, s0):
    b, T, H, dk = q.shape
    dv = v.shape[-1]
    L = math.gcd(T, GDN_CHUNK)
    c = T // L
    q = q.reshape(b, c, L, H, dk)
    k = k.reshape(b, c, L, H, dk)
    v = v.reshape(b, c, L, H, dv)
    g = g.reshape(b, c, L, H)
    beta = beta.reshape(b, c, L, H)
    gc = jnp.cumsum(g, axis=2)
    gt = jnp.swapaxes(gc, 2, 3)
    incl = jnp.tril(jnp.ones((L, L), dtype=bool))
    strict = jnp.tril(jnp.ones((L, L), dtype=bool), -1)
    decay = jnp.exp(jnp.where(incl, gt[..., :, None] - gt[..., None, :], -jnp.inf))
    kb = k * beta[..., None]
    m = jnp.where(strict, jnp.einsum('bcihd,bcjhd->bchij', kb, k) * decay, 0.0)
    eye = jnp.eye(L, dtype=jnp.float32)
    t_inv = lax.linalg.triangular_solve(m + eye, jnp.broadcast_to(eye, m.shape), left_side=True, lower=True)
    u = jnp.einsum('bchij,bcjhd->bcihd', t_inv, v * beta[..., None])
    w = jnp.einsum('bchij,bcjhd->bcihd', t_inv, kb * jnp.exp(gc)[..., None])
    a_intra = jnp.einsum('bcihd,bcjhd->bchij', q, k) * decay

    def step(s, inp):
        qc, kc, uc, wc, ac, gcc = inp
        v_new = uc - jnp.einsum('bihd,bhde->bihe', wc, s)
        o = jnp.einsum('bihd,bhde->bihe', qc * jnp.exp(gcc)[..., None], s) + jnp.einsum('bhij,bjhe->bihe', ac, v_new)
        g_last = gcc[:, -1]
        k_dec = kc * jnp.exp(g_last[:, None] - gcc)[..., None]
        s = s * jnp.exp(g_last)[:, :, None, None] + jnp.einsum('bjhd,bjhe->bhde', k_dec, v_new)
        return s, o

    xs = tuple(jnp.moveaxis(t, 1, 0) for t in (q, k, u, w, a_intra, gc))
    s_final, o = lax.scan(step, s0, xs)
    return jnp.moveaxis(o, 0, 1).reshape(b, T, H, dv), s_final


def gdn_mixer(h, conv_prev, s_prev, w_in, conv_w, dt_bias, a_log, norm_g, w_out):
    b, T, _ = h.shape
    f32 = jnp.float32
    c1 = GDN_CONV_DIM
    c2 = c1 + GDN_VAL_DIM
    c3 = c2 + GDN_V_HEADS
    qkv, z, beta_raw, a_raw = jnp.split(h @ w_in, [c1, c2, c3], axis=-1)
    qkv, conv_new = causal_conv(qkv, conv_prev, conv_w)
    qkv = jax.nn.silu(qkv)
    q, k, v = jnp.split(qkv, [GDN_KEY_DIM, 2 * GDN_KEY_DIM], axis=-1)
    rep = GDN_V_HEADS // GDN_K_HEADS
    q = jnp.repeat(l2norm(q.reshape(b, T, GDN_K_HEADS, GDN_DK)), rep, axis=2) * GDN_DK ** -0.5
    k = jnp.repeat(l2norm(k.reshape(b, T, GDN_K_HEADS, GDN_DK)), rep, axis=2)
    v = v.reshape(b, T, GDN_V_HEADS, GDN_DV).astype(f32)
    beta = jax.nn.sigmoid(beta_raw.astype(f32))
    g = -jnp.exp(a_log.astype(f32)) * jax.nn.softplus(a_raw.astype(f32) + dt_bias.astype(f32))
    o, s_new = gated_delta_chunked(q, k, v, g, beta, s_prev.astype(f32))
    o = rmsnorm(o, norm_g) * jax.nn.silu(z.reshape(b, T, GDN_V_HEADS, GDN_DV).astype(f32))
    return o.reshape(b, T, GDN_VAL_DIM).astype(h.dtype) @ w_out, conv_new, s_new


def nsa_project(h, w_in):
    b, T, _ = h.shape
    R = ATT_HEADS // NSA_KV
    qd, kvd = ATT_HEADS * HEAD_DIM, NSA_KV * HEAD_DIM
    parts = jnp.split(h @ w_in, [qd + j * kvd for j in range(7)], axis=-1)
    q = parts[0].reshape(b, T, NSA_KV, R, HEAD_DIM) * HEAD_DIM ** -0.5
    kvs = [t.reshape(b, T, NSA_KV, HEAD_DIM) for t in parts[1:7]]
    gates = jax.nn.sigmoid(parts[7].astype(jnp.float32)).reshape(b, T, NSA_KV, R, 3)
    return q, kvs, gates


def compress(x, pos, w1, w2):
    b, n, G, hd = x.shape
    nb = n // NSA_BLOCK
    xb = x[:, :nb * NSA_BLOCK].reshape(b, nb, NSA_BLOCK, G, hd) + pos[:, None, :]
    xb = jnp.swapaxes(xb, 2, 3).reshape(b, nb, G, NSA_BLOCK * hd)
    return jax.nn.silu(xb @ w1) @ w2


def nsa_compressed(q, kc, vc, qpos, table):
    b, T, G, R, hd = q.shape
    nC = kc.shape[1]
    logits = jnp.einsum('btgrd,bjgd->bgrtj', q, kc).astype(jnp.float32)
    dist = qpos[:, None] - (jnp.arange(nC) * NSA_BLOCK + NSA_BLOCK - 1)[None, :]
    logits = logits + table.T[:, rel_bucket(dist)].reshape(G, R, T, nC).astype(jnp.float32)
    p = masked_softmax(logits, dist >= 0)
    return jnp.einsum('bgrtj,bjgd->btgrd', p.astype(vc.dtype), vc), p


def nsa_select(p_slc, qpos, n_sel):
    nC = p_slc.shape[-1]
    width = max(n_sel, NSA_TOPN)
    score = jnp.pad(p_slc, ((0, 0), (0, 0), (0, 0), (0, width - nC)))
    blk = jnp.arange(width)[None, :]
    cur = (qpos // NSA_BLOCK)[:, None]
    valid = blk <= cur
    forced = valid & ((blk == 0) | (blk >= cur - 1))
    score = jnp.where(forced, NSA_FORCE, jnp.where(valid, score, -1.0))
    _, idx = lax.top_k(score, NSA_TOPN)
    return idx, idx <= cur[None, None]


def nsa_sel_attend(q, kg, vg, idx, valid, qpos, table):
    b, T, G, R, hd = q.shape
    n = idx.shape[-1]
    logits = jnp.einsum('btgrd,bgtnkd->bgtrnk', q, kg).astype(jnp.float32)
    kpos = idx[..., None] * NSA_BLOCK + jnp.arange(NSA_BLOCK)
    dist = qpos[None, None, :, None, None] - kpos
    mask = (dist >= 0) & valid[..., None]
    tbl = table.reshape(REL_BUCKETS, G, R)
    gi = jnp.arange(G)[None, :, None, None, None]
    bias = jnp.moveaxis(tbl[rel_bucket(dist), gi], -1, 3).astype(jnp.float32)
    logits = (logits + bias).reshape(b, G, T, R, n * NSA_BLOCK)
    p = masked_softmax(logits, mask.reshape(b, G, T, 1, n * NSA_BLOCK))
    return jnp.einsum('bgtrk,bgtkd->btgrd', p.astype(vg.dtype), vg.reshape(b, G, T, n * NSA_BLOCK, hd))


def nsa_sel_prompt(q, ks, vs, idx, valid, table):
    b, T, G, R, hd = q.shape
    nblk = T // NSA_BLOCK
    nq = T // ATT_BLOCK
    kblk = ks.reshape(b, nblk, NSA_BLOCK, G, hd)
    vblk = vs.reshape(b, nblk, NSA_BLOCK, G, hd)
    bi = jnp.arange(b)[:, None, None, None]
    gi = jnp.arange(G)[None, :, None, None]
    idx_c = jnp.minimum(idx, nblk - 1)

    def one(args):
        qq, ii, iic, vv, qp = args
        return nsa_sel_attend(qq, kblk[bi, iic, :, gi], vblk[bi, iic, :, gi], ii, vv, qp, table)

    to_blocks = lambda t: jnp.moveaxis(t.reshape(b, G, nq, ATT_BLOCK, NSA_TOPN), 2, 0)
    qs = jnp.moveaxis(q.reshape(b, nq, ATT_BLOCK, G, R, hd), 1, 0)
    qp = jnp.arange(T).reshape(nq, ATT_BLOCK)
    out = lax.map(one, (qs, to_blocks(idx), to_blocks(idx_c), to_blocks(valid), qp))
    return jnp.moveaxis(out, 0, 1).reshape(b, T, G, R, hd)


def nsa_sel_sample(q, pool_k, pool_v, new_k, new_v, page_table, idx, valid, qpos, table):
    b, T, G, R, hd = q.shape
    sub = PAGE_SIZE // NSA_BLOCK
    n_past_blk = page_table.shape[1] * sub
    n_new_blk = -(-T // NSA_BLOCK)
    pad = n_new_blk * NSA_BLOCK - T
    bi = jnp.arange(b)[:, None, None, None]
    gi = jnp.arange(G)[None, :, None, None]
    from_past = (idx < n_past_blk)[..., None, None]
    pidx = jnp.minimum(idx, n_past_blk - 1)
    phys = page_table[bi, pidx // sub] * sub + pidx % sub
    nidx = jnp.clip(idx - n_past_blk, 0, n_new_blk - 1)

    def gather(pool, new):
        pool_blk = pool.reshape(-1, NSA_BLOCK, G, hd)
        new_blk = jnp.pad(new, ((0, 0), (0, pad), (0, 0), (0, 0))).reshape(b, n_new_blk, NSA_BLOCK, G, hd)
        return jnp.where(from_past, pool_blk[phys, :, gi].astype(new.dtype), new_blk[bi, nidx, :, gi])

    return nsa_sel_attend(q, gather(pool_k, new_k), gather(pool_v, new_v), idx, valid, qpos, table)


def nsa_out(gates, o_cmp, o_sel, o_win, w_o, dtype):
    o = gates[..., 0:1] * o_cmp + gates[..., 1:2] * o_sel + gates[..., 2:3] * o_win
    b, T = o.shape[:2]
    return o.reshape(b, T, -1).astype(dtype) @ w_o


def nsa_prompt(h, rel_table, w_in, pos_k, pos_v, w1_k, w2_k, w1_v, w2_v, w_o):
    b, T, _ = h.shape
    q, (kc, vc, ks, vs, kw, vw), gates = nsa_project(h, w_in)
    qpos = jnp.arange(T)
    o_cmp, p_cmp = nsa_compressed(q, compress(kc, pos_k, w1_k, w2_k), compress(vc, pos_v, w1_v, w2_v), qpos, rel_table)
    idx, valid = nsa_select(jnp.sum(p_cmp, axis=2), qpos, T // NSA_BLOCK)
    o_sel = nsa_sel_prompt(q, ks, vs, idx, valid, rel_table)
    o_win = banded_attention(q, kw, vw, NSA_WINDOW, rel_table, None)
    out = nsa_out(gates, o_cmp, o_sel, o_win, w_o, h.dtype)
    buf = min(NSA_WINDOW, T)
    return out, kc, vc, ks, vs, kw[:, T - buf:], vw[:, T - buf:]


def nsa_sample(h, cmp_k_pool, cmp_v_pool, sel_k_pool, sel_v_pool, win_k, win_v, page_table, rel_table,
               w_in, pos_k, pos_v, w1_k, w2_k, w1_v, w2_v, w_o):
    b, T, _ = h.shape
    q, (kc, vc, ks, vs, kw, vw), gates = nsa_project(h, w_in)
    past = page_table.shape[1] * PAGE_SIZE
    qpos = past + jnp.arange(T)

    def full_rows(pool, new):
        rows = pool[page_table].reshape(b, past, NSA_KV, HEAD_DIM)
        return jnp.concatenate([rows.astype(new.dtype), new], axis=1)

    kc_blk = compress(full_rows(cmp_k_pool, kc), pos_k, w1_k, w2_k)
    vc_blk = compress(full_rows(cmp_v_pool, vc), pos_v, w1_v, w2_v)
    o_cmp, p_cmp = nsa_compressed(q, kc_blk, vc_blk, qpos, rel_table)
    idx, valid = nsa_select(jnp.sum(p_cmp, axis=2), qpos, -(-(past + T) // NSA_BLOCK))
    o_sel = nsa_sel_sample(q, sel_k_pool, sel_v_pool, ks, vs, page_table, idx, valid, qpos, rel_table)
    buf = win_k.shape[1]
    kk = jnp.concatenate([win_k.astype(kw.dtype), kw], axis=1)
    vv = jnp.concatenate([win_v.astype(vw.dtype), vw], axis=1)
    kpos = past - buf + jnp.arange(buf + T)
    o_win = local_attend(q, kk, vv, qpos, kpos, NSA_WINDOW, rel_table, None)
    out = nsa_out(gates, o_cmp, o_sel, o_win, w_o, h.dtype)
    return out, kc, vc, ks, vs, kk[:, T:], vv[:, T:]


def setup_inputs(seed: int = 0) -> dict:
    key = jax.random.key(seed)
    ks = iter(jax.random.split(key, 64))
    f32 = jnp.float32

    def nrm(shape, scale=1.0):
        return jax.random.normal(next(ks), shape, f32) * scale

    def gain(shape):
        return 1.0 + nrm(shape, 0.05)

    def dt_bias(shape):
        u = jax.random.uniform(next(ks), shape, f32)
        dt = jnp.exp(u * (math.log(0.1) - math.log(0.001)) + math.log(0.001))
        return dt + jnp.log(-jnp.expm1(-dt))

    def a_log(shape):
        return jnp.log(jax.random.uniform(next(ks), shape, f32, 1.0, 16.0))

    n_pages = PAST_LEN // PAGE_SIZE
    n_pool = (DEC_BATCH * n_pages * 5) // 4
    swa_buf = min(SWA_WINDOW, PAST_LEN)
    nsa_buf = min(NSA_WINDOW, PAST_LEN)
    qd = ATT_HEADS * HEAD_DIM
    ssd_in = 2 * SSD_D_INNER + 2 * SSD_GROUPS * SSD_STATE + SSD_HEADS
    gdn_in = GDN_CONV_DIM + GDN_VAL_DIM + 2 * GDN_V_HEADS
    nsa_in = qd + 6 * NSA_KV * HEAD_DIM + 3 * ATT_HEADS
    paged = (N_NSA, n_pool, PAGE_SIZE, NSA_KV, HEAD_DIM)
    return {
        'x_prompt': nrm((BATCH, SEQ, D_MODEL)),
        'x_sample': nrm((DEC_BATCH, DEC_SEQ, D_MODEL)),
        'state_ssd_conv': nrm((N_SSD, DEC_BATCH, SSD_CONV - 1, SSD_CONV_DIM)),
        'state_ssd': nrm((N_SSD, DEC_BATCH, SSD_HEADS, SSD_HEADDIM, SSD_STATE), 0.1),
        'cache_swa_k': nrm((N_SWA, DEC_BATCH, swa_buf, SWA_KV, HEAD_DIM)),
        'cache_swa_v': nrm((N_SWA, DEC_BATCH, swa_buf, SWA_KV, HEAD_DIM)),
        'state_gdn_conv': nrm((N_GDN, DEC_BATCH, GDN_CONV - 1, GDN_CONV_DIM)),
        'state_gdn': nrm((N_GDN, DEC_BATCH, GDN_V_HEADS, GDN_DK, GDN_DV), 0.1),
        'cache_nsa_cmp_k': nrm(paged),
        'cache_nsa_cmp_v': nrm(paged),
        'cache_nsa_sel_k': nrm(paged),
        'cache_nsa_sel_v': nrm(paged),
        'cache_nsa_win_k': nrm((N_NSA, DEC_BATCH, nsa_buf, NSA_KV, HEAD_DIM)),
        'cache_nsa_win_v': nrm((N_NSA, DEC_BATCH, nsa_buf, NSA_KV, HEAD_DIM)),
        'page_table': jax.random.permutation(next(ks), n_pool)[:DEC_BATCH * n_pages].reshape(DEC_BATCH, n_pages).astype(jnp.int32),
        'rel_table': nrm((REL_BUCKETS, ATT_HEADS), 0.5),
        'norm_ffn1': gain((DEPTH, D_MODEL)),
        'norm_mix': gain((DEPTH, D_MODEL)),
        'norm_ffn2': gain((DEPTH, D_MODEL)),
        'norm_final': gain((D_MODEL,)),
        'ffn1_gate': nrm((DEPTH, D_MODEL, D_FF), D_MODEL ** -0.5),
        'ffn1_up': nrm((DEPTH, D_MODEL, D_FF), D_MODEL ** -0.5),
        'ffn1_down': nrm((DEPTH, D_FF, D_MODEL), D_FF ** -0.5),
        'ffn2_gate': nrm((DEPTH, D_MODEL, D_FF), D_MODEL ** -0.5),
        'ffn2_up': nrm((DEPTH, D_MODEL, D_FF), D_MODEL ** -0.5),
        'ffn2_down': nrm((DEPTH, D_FF, D_MODEL), D_FF ** -0.5),
        'ssd_w_in': nrm((N_SSD, D_MODEL, ssd_in), D_MODEL ** -0.5),
        'ssd_conv_w': nrm((N_SSD, SSD_CONV, SSD_CONV_DIM), SSD_CONV ** -0.5),
        'ssd_conv_b': nrm((N_SSD, SSD_CONV_DIM), 0.01),
        'ssd_dt_bias': dt_bias((N_SSD, SSD_HEADS)),
        'ssd_a_log': a_log((N_SSD, SSD_HEADS)),
        'ssd_d': gain((N_SSD, SSD_HEADS)),
        'ssd_norm': gain((N_SSD, SSD_D_INNER)),
        'ssd_w_out': nrm((N_SSD, SSD_D_INNER, D_MODEL), SSD_D_INNER ** -0.5),
        'swa_w_qkv': nrm((N_SWA, D_MODEL, qd + 2 * SWA_KV * HEAD_DIM), D_MODEL ** -0.5),
        'swa_sink': nrm((N_SWA, ATT_HEADS)),
        'swa_w_o': nrm((N_SWA, qd, D_MODEL), qd ** -0.5),
        'gdn_w_in': nrm((N_GDN, D_MODEL, gdn_in), D_MODEL ** -0.5),
        'gdn_conv_w': nrm((N_GDN, GDN_CONV, GDN_CONV_DIM), GDN_CONV ** -0.5),
        'gdn_dt_bias': dt_bias((N_GDN, GDN_V_HEADS)),
        'gdn_a_log': a_log((N_GDN, GDN_V_HEADS)),
        'gdn_norm': gain((N_GDN, GDN_DV)),
        'gdn_w_out': nrm((N_GDN, GDN_VAL_DIM, D_MODEL), GDN_VAL_DIM ** -0.5),
        'nsa_w_in': nrm((N_NSA, D_MODEL, nsa_in), D_MODEL ** -0.5),
        'nsa_pos_k': nrm((N_NSA, NSA_BLOCK, HEAD_DIM), 0.02),
        'nsa_pos_v': nrm((N_NSA, NSA_BLOCK, HEAD_DIM), 0.02),
        'nsa_cmp_w1_k': nrm((N_NSA, NSA_BLOCK * HEAD_DIM, NSA_CMP_HID), (NSA_BLOCK * HEAD_DIM) ** -0.5),
        'nsa_cmp_w2_k': nrm((N_NSA, NSA_CMP_HID, HEAD_DIM), NSA_CMP_HID ** -0.5),
        'nsa_cmp_w1_v': nrm((N_NSA, NSA_BLOCK * HEAD_DIM, NSA_CMP_HID), (NSA_BLOCK * HEAD_DIM) ** -0.5),
        'nsa_cmp_w2_v': nrm((N_NSA, NSA_CMP_HID, HEAD_DIM), NSA_CMP_HID ** -0.5),
        'nsa_w_o': nrm((N_NSA, qd, D_MODEL), qd ** -0.5),
    }


def reference(x_prompt, x_sample, state_ssd_conv, state_ssd, cache_swa_k, cache_swa_v, state_gdn_conv, state_gdn,
              cache_nsa_cmp_k, cache_nsa_cmp_v, cache_nsa_sel_k, cache_nsa_sel_v, cache_nsa_win_k, cache_nsa_win_v,
              page_table, rel_table, norm_ffn1, norm_mix, norm_ffn2, norm_final,
              ffn1_gate, ffn1_up, ffn1_down, ffn2_gate, ffn2_up, ffn2_down,
              ssd_w_in, ssd_conv_w, ssd_conv_b, ssd_dt_bias, ssd_a_log, ssd_d, ssd_norm, ssd_w_out,
              swa_w_qkv, swa_sink, swa_w_o,
              gdn_w_in, gdn_conv_w, gdn_dt_bias, gdn_a_log, gdn_norm, gdn_w_out,
              nsa_w_in, nsa_pos_k, nsa_pos_v, nsa_cmp_w1_k, nsa_cmp_w2_k, nsa_cmp_w1_v, nsa_cmp_w2_v, nsa_w_o):
    bp = x_prompt.shape[0]
    yp, ys = x_prompt, x_sample
    ssd_conv_p, ssd_state_p, ssd_conv_s, ssd_state_s = [], [], [], []
    swa_k_p, swa_v_p, swa_k_s, swa_v_s = [], [], [], []
    gdn_conv_p, gdn_state_p, gdn_conv_s, gdn_state_s = [], [], [], []
    nsa_p = [[] for _ in range(6)]
    nsa_s = [[] for _ in range(6)]
    for i in range(DEPTH):
        kind, li = i % N_MIXERS, i // N_MIXERS
        yp = yp + 0.5 * swiglu(rmsnorm(yp, norm_ffn1[i]), ffn1_gate[i], ffn1_up[i], ffn1_down[i])
        ys = ys + 0.5 * swiglu(rmsnorm(ys, norm_ffn1[i]), ffn1_gate[i], ffn1_up[i], ffn1_down[i])
        hp, hs = rmsnorm(yp, norm_mix[i]), rmsnorm(ys, norm_mix[i])
        if kind == 0:
            w = (ssd_w_in[li], ssd_conv_w[li], ssd_conv_b[li], ssd_dt_bias[li], ssd_a_log[li], ssd_d[li], ssd_norm[li], ssd_w_out[li])
            mp, cp, sp = ssd_mixer(hp, jnp.zeros((bp, SSD_CONV - 1, SSD_CONV_DIM), hp.dtype),
                                   jnp.zeros((bp, SSD_HEADS, SSD_HEADDIM, SSD_STATE), jnp.float32), *w)
            ms, cs, ss = ssd_mixer(hs, state_ssd_conv[li], state_ssd[li], *w)
            ssd_conv_p.append(cp)
            ssd_state_p.append(sp)
            ssd_conv_s.append(cs)
            ssd_state_s.append(ss)
        elif kind == 1:
            mp, kp, vp = swa_prompt(hp, rel_table, swa_w_qkv[li], swa_sink[li], swa_w_o[li])
            ms, kq, vq = swa_sample(hs, cache_swa_k[li], cache_swa_v[li], rel_table, swa_w_qkv[li], swa_sink[li], swa_w_o[li])
            swa_k_p.append(kp)
            swa_v_p.append(vp)
            swa_k_s.append(kq)
            swa_v_s.append(vq)
        elif kind == 2:
            w = (gdn_w_in[li], gdn_conv_w[li], gdn_dt_bias[li], gdn_a_log[li], gdn_norm[li], gdn_w_out[li])
            mp, cp, sp = gdn_mixer(hp, jnp.zeros((bp, GDN_CONV - 1, GDN_CONV_DIM), hp.dtype),
                                   jnp.zeros((bp, GDN_V_HEADS, GDN_DK, GDN_DV), jnp.float32), *w)
            ms, cs, ss = gdn_mixer(hs, state_gdn_conv[li], state_gdn[li], *w)
            gdn_conv_p.append(cp)
            gdn_state_p.append(sp)
            gdn_conv_s.append(cs)
            gdn_state_s.append(ss)
        else:
            w = (nsa_w_in[li], nsa_pos_k[li], nsa_pos_v[li], nsa_cmp_w1_k[li], nsa_cmp_w2_k[li],
                 nsa_cmp_w1_v[li], nsa_cmp_w2_v[li], nsa_w_o[li])
            outp = nsa_prompt(hp, rel_table, *w)
            outs = nsa_sample(hs, cache_nsa_cmp_k[li], cache_nsa_cmp_v[li], cache_nsa_sel_k[li], cache_nsa_sel_v[li],
                              cache_nsa_win_k[li], cache_nsa_win_v[li], page_table, rel_table, *w)
            mp, ms = outp[0], outs[0]
            for j in range(6):
                nsa_p[j].append(outp[j + 1])
                nsa_s[j].append(outs[j + 1])
        yp = yp + mp.astype(yp.dtype)
        ys = ys + ms.astype(ys.dtype)
        yp = yp + 0.5 * swiglu(rmsnorm(yp, norm_ffn2[i]), ffn2_gate[i], ffn2_up[i], ffn2_down[i])
        ys = ys + 0.5 * swiglu(rmsnorm(ys, norm_ffn2[i]), ffn2_gate[i], ffn2_up[i], ffn2_down[i])
    y_prompt = rmsnorm(yp, norm_final)
    y_sample = rmsnorm(ys, norm_final)
    new_ssd_conv_prompt = jnp.stack(ssd_conv_p)
    new_ssd_state_prompt = jnp.stack(ssd_state_p)
    new_swa_k_prompt = jnp.stack(swa_k_p)
    new_swa_v_prompt = jnp.stack(swa_v_p)
    new_gdn_conv_prompt = jnp.stack(gdn_conv_p)
    new_gdn_state_prompt = jnp.stack(gdn_state_p)
    new_nsa_cmp_k_prompt = jnp.stack(nsa_p[0])
    new_nsa_cmp_v_prompt = jnp.stack(nsa_p[1])
    new_nsa_sel_k_prompt = jnp.stack(nsa_p[2])
    new_nsa_sel_v_prompt = jnp.stack(nsa_p[3])
    new_nsa_win_k_prompt = jnp.stack(nsa_p[4])
    new_nsa_win_v_prompt = jnp.stack(nsa_p[5])
    new_ssd_conv_sample = jnp.stack(ssd_conv_s)
    new_ssd_state_sample = jnp.stack(ssd_state_s)
    new_swa_k_sample = jnp.stack(swa_k_s)
    new_swa_v_sample = jnp.stack(swa_v_s)
    new_gdn_conv_sample = jnp.stack(gdn_conv_s)
    new_gdn_state_sample = jnp.stack(gdn_state_s)
    new_nsa_cmp_k_sample = jnp.stack(nsa_s[0])
    new_nsa_cmp_v_sample = jnp.stack(nsa_s[1])
    new_nsa_sel_k_sample = jnp.stack(nsa_s[2])
    new_nsa_sel_v_sample = jnp.stack(nsa_s[3])
    new_nsa_win_k_sample = jnp.stack(nsa_s[4])
    new_nsa_win_v_sample = jnp.stack(nsa_s[5])
    return (y_prompt, y_sample,
            new_ssd_conv_prompt, new_ssd_state_prompt, new_swa_k_prompt, new_swa_v_prompt,
            new_gdn_conv_prompt, new_gdn_state_prompt, new_nsa_cmp_k_prompt, new_nsa_cmp_v_prompt,
            new_nsa_sel_k_prompt, new_nsa_sel_v_prompt, new_nsa_win_k_prompt, new_nsa_win_v_prompt,
            new_ssd_conv_sample, new_ssd_state_sample, new_swa_k_sample, new_swa_v_sample,
            new_gdn_conv_sample, new_gdn_state_sample, new_nsa_cmp_k_sample, new_nsa_cmp_v_sample,
            new_nsa_sel_k_sample, new_nsa_sel_v_sample, new_nsa_win_k_sample, new_nsa_win_v_sample)
```

```python
import functools
import math

import jax
import jax.numpy as jnp
from jax import lax
from jax.experimental import pallas as pl
from jax.experimental.pallas import tpu as pltpu

F32 = jnp.float32
BF16 = jnp.bfloat16
HIGHEST = lax.Precision.HIGHEST

EPS = 1e-6
D_MODEL = 2048
D_FF = 5504
LANE = 128
VMEM_LIMIT = 56 * 1024 * 1024
NEG = -1e30

FF_TILE = 512
D_FF_PAD = -(-D_FF // FF_TILE) * FF_TILE
ROW_TILE = 512
COL_TILE = 512

SSD_INNER = 4096
SSD_HEADS = 64
SSD_P = 64
SSD_N = 128
SSD_GROUPS = 8
SSD_GW = SSD_INNER // SSD_GROUPS
SSD_CONV_DIM = SSD_INNER + 2 * SSD_GROUPS * SSD_N
SSD_CHUNK = 128


def _cparams(*sem):
    return pltpu.CompilerParams(dimension_semantics=sem, vmem_limit_bytes=VMEM_LIMIT)


def _pad_cols(w, n):
    return jnp.pad(w, ((0, 0), (0, n - w.shape[1])))


def _round_up(n, m):
    return -(-n // m) * m


def _sigmoid(x):
    return 1.0 / (1.0 + jnp.exp(-x))


def _silu(x):
    return x * _sigmoid(x)


def _softplus(x):
    return jnp.maximum(x, 0.0) + jnp.log(1.0 + jnp.exp(-jnp.abs(x)))


def _rms(x, g):
    return x * lax.rsqrt(jnp.mean(x * x, axis=-1, keepdims=True) + EPS) * g


def _dot(a, b, **kw):
    return jnp.dot(a, b, preferred_element_type=F32, **kw)


def _dot_nt(a, b):
    return lax.dot_general(a, b, (((1,), (1,)), ((), ())), preferred_element_type=F32)


def _dot_tn(a, b):
    return lax.dot_general(a, b, (((0,), (0,)), ((), ())), preferred_element_type=F32)


def _eye(n):
    r = lax.broadcasted_iota(jnp.int32, (n, n), 0)
    c = lax.broadcasted_iota(jnp.int32, (n, n), 1)
    return r == c


def _row_to_col(v, eye):
    n = v.shape[1]
    return jnp.sum(jnp.where(eye, jnp.broadcast_to(v, (n, n)), 0.0), axis=-1, keepdims=True)


def _col_to_row(v, eye):
    n = v.shape[0]
    return jnp.sum(jnp.where(eye, jnp.broadcast_to(v, (n, n)), 0.0), axis=0, keepdims=True)


def _ffn_kernel(x_ref, g_ref, wg_ref, wu_ref, wd_ref, o_ref, xn_ref, acc_ref):
    f = pl.program_id(1)

    @pl.when(f == 0)
    def _():
        xn_ref[...] = _rms(x_ref[...], g_ref[...]).astype(BF16)
        acc_ref[...] = jnp.zeros_like(acc_ref)

    xn = xn_ref[...]
    gate = _dot(xn, wg_ref[...])
    up = _dot(xn, wu_ref[...])
    h = (_silu(gate) * up).astype(BF16)
    acc_ref[...] += _dot(h, wd_ref[...])

    @pl.when(f == pl.num_programs(1) - 1)
    def _():
        o_ref[...] = x_ref[...] + 0.5 * acc_ref[...]


def ffn(x, g, wg, wu, wd):
    m, d = x.shape
    tm = min(ROW_TILE, m)
    return pl.pallas_call(
        _ffn_kernel,
        grid=(m // tm, D_FF_PAD // FF_TILE),
        in_specs=[
            pl.BlockSpec((tm, d), lambda i, f: (i, 0)),
            pl.BlockSpec((1, d), lambda i, f: (0, 0)),
            pl.BlockSpec((d, FF_TILE), lambda i, f: (0, f)),
            pl.BlockSpec((d, FF_TILE), lambda i, f: (0, f)),
            pl.BlockSpec((FF_TILE, d), lambda i, f: (f, 0)),
        ],
        out_specs=pl.BlockSpec((tm, d), lambda i, f: (i, 0)),
        out_shape=jax.ShapeDtypeStruct((m, d), F32),
        scratch_shapes=[pltpu.VMEM((tm, d), BF16), pltpu.VMEM((tm, d), F32)],
        compiler_params=_cparams("parallel", "arbitrary"),
        name="ffn",
    )(x, g.reshape(1, d), wg, wu, wd)


def _norm_proj_kernel(x_ref, g_ref, w_ref, o_ref, xn_ref):
    @pl.when(pl.program_id(1) == 0)
    def _():
        xn_ref[...] = _rms(x_ref[...], g_ref[...]).astype(BF16)

    o_ref[...] = _dot(xn_ref[...], w_ref[...])


def norm_proj(x, g, w):
    m, d = x.shape
    n = w.shape[1]
    tm = min(ROW_TILE, m)
    return pl.pallas_call(
        _norm_proj_kernel,
        grid=(m // tm, n // COL_TILE),
        in_specs=[
            pl.BlockSpec((tm, d), lambda i, j: (i, 0)),
            pl.BlockSpec((1, d), lambda i, j: (0, 0)),
            pl.BlockSpec((d, COL_TILE), lambda i, j: (0, j)),
        ],
        out_specs=pl.BlockSpec((tm, COL_TILE), lambda i, j: (i, j)),
        out_shape=jax.ShapeDtypeStruct((m, n), F32),
        scratch_shapes=[pltpu.VMEM((tm, d), BF16)],
        compiler_params=_cparams("parallel", "arbitrary"),
        name="norm_proj",
    )(x, g.reshape(1, d), w)


def _proj_res_kernel(a_ref, w_ref, r_ref, o_ref):
    o_ref[...] = r_ref[...] + _dot(a_ref[...], w_ref[...])


def proj_res(a, w, res):
    m, k = a.shape
    n = w.shape[1]
    tm = min(ROW_TILE, m)
    return pl.pallas_call(
        _proj_res_kernel,
        grid=(m // tm, n // COL_TILE),
        in_specs=[
            pl.BlockSpec((tm, k), lambda i, j: (i, 0)),
            pl.BlockSpec((k, COL_TILE), lambda i, j: (0, j)),
            pl.BlockSpec((tm, COL_TILE), lambda i, j: (i, j)),
        ],
        out_specs=pl.BlockSpec((tm, COL_TILE), lambda i, j: (i, j)),
        out_shape=jax.ShapeDtypeStruct((m, n), F32),
        compiler_params=_cparams("parallel", "parallel"),
        name="proj_res",
    )(a, w, res)


def _final_norm_kernel(x_ref, g_ref, o_ref):
    o_ref[...] = _rms(x_ref[...], g_ref[...])


def final_norm(x, g):
    m, d = x.shape
    tm = min(ROW_TILE, m)
    return pl.pallas_call(
        _final_norm_kernel,
        grid=(m // tm,),
        in_specs=[pl.BlockSpec((tm, d), lambda i: (i, 0)), pl.BlockSpec((1, d), lambda i: (0, 0))],
        out_specs=pl.BlockSpec((tm, d), lambda i: (i, 0)),
        out_shape=jax.ShapeDtypeStruct((m, d), F32),
        compiler_params=_cparams("parallel"),
        name="final_norm",
    )(x, g.reshape(1, d))


def _head_expand(width):
    r = lax.broadcasted_iota(jnp.int32, (LANE, width), 0)
    c = lax.broadcasted_iota(jnp.int32, (LANE, width), 1)
    return jnp.where(lax.shift_right_logical(c, 6) == r, 1.0, 0.0).astype(F32)


def _ssd_prompt_kernel(z_ref, x_ref, bc_ref, dt_ref, cw_ref, cb_ref, dtb_ref, alog_ref, dx_ref, ng_ref,
                       y_ref, st_ref, xpad_ref, s_ref, ybuf_ref):
    c = pl.program_id(1)
    L = SSD_CHUNK

    @pl.when(c == 0)
    def _():
        xpad_ref[0:8, :] = jnp.zeros((8, SSD_CONV_DIM), F32)
        s_ref[...] = jnp.zeros_like(s_ref)

    xpad_ref[8:8 + L, 0:SSD_INNER] = x_ref[...]
    xpad_ref[8:8 + L, SSD_INNER:SSD_CONV_DIM] = bc_ref[...]
    conv = cb_ref[...] + cw_ref[3:4, :] * xpad_ref[8:8 + L, :]
    for j in range(3):
        conv = conv + cw_ref[j:j + 1, :] * xpad_ref[pl.ds(5 + j, L), :]
    xpad_ref[0:8, :] = xpad_ref[L:L + 8, :]
    act = _silu(conv)
    xs = act[:, 0:SSD_INNER]
    bm = act[:, SSD_INNER:SSD_INNER + SSD_GROUPS * SSD_N]
    cm = act[:, SSD_INNER + SSD_GROUPS * SSD_N:SSD_CONV_DIM]

    dt = _softplus(dt_ref[...] + dtb_ref[...])
    da = dt * (-jnp.exp(alog_ref[...]))
    row = lax.broadcasted_iota(jnp.int32, (L, L), 0)
    col = lax.broadcasted_iota(jnp.int32, (L, L), 1)
    causal = row >= col
    cs = _dot(jnp.where(causal, 1.0, 0.0).astype(F32), da, precision=HIGHEST)
    cs_t = cs.T
    expand = _head_expand(SSD_INNER)
    dtx = _dot(dt, expand, precision=HIGHEST)
    csx = _dot(cs, expand, precision=HIGHEST)
    xdt = xs * dtx
    cs_last = csx[L - 1:L, :]
    xdt_end = (xdt * jnp.exp(cs_last - csx)).astype(BF16)
    xdt_b = xdt.astype(BF16)
    ecs = jnp.exp(csx)
    dec_last = jnp.exp(cs_last)
    lane = lax.broadcasted_iota(jnp.int32, (L, LANE), 1)

    for g in range(SSD_GROUPS):
        lo_n = g * SSD_N
        lo_f = g * SSD_GW
        bg = bm[:, lo_n:lo_n + SSD_N]
        cg = cm[:, lo_n:lo_n + SSD_N].astype(BF16)
        cb = _dot_nt(cg, bg.astype(BF16))
        sg = s_ref[:, lo_f:lo_f + SSD_GW]
        yoff = _dot(cg, sg.astype(BF16)) * ecs[:, lo_f:lo_f + SSD_GW]
        for pr in range(SSD_GW // LANE):
            lo = lo_f + pr * LANE
            xp = xdt_b[:, lo:lo + LANE]
            outs = []
            for hh in (g * 8 + 2 * pr, g * 8 + 2 * pr + 1):
                diff = cs[:, hh:hh + 1] - cs_t[hh:hh + 1, :]
                dec = jnp.exp(jnp.where(causal, diff, NEG))
                outs.append(_dot((cb * dec).astype(BF16), xp))
            ybuf_ref[:, lo:lo + LANE] = jnp.where(lane < SSD_P, outs[0], outs[1]) + yoff[:, pr * LANE:(pr + 1) * LANE]
        s_ref[:, lo_f:lo_f + SSD_GW] = (sg * dec_last[:, lo_f:lo_f + SSD_GW]
                                        + _dot(bg.T.astype(BF16), xdt_end[:, lo_f:lo_f + SSD_GW]))

    y = ybuf_ref[...] + xs * dx_ref[...]
    yz = y * _silu(z_ref[...])
    for g in range(SSD_GROUPS):
        lo_f = g * SSD_GW
        y_ref[:, lo_f:lo_f + SSD_GW] = _rms(yz[:, lo_f:lo_f + SSD_GW], ng_ref[:, lo_f:lo_f + SSD_GW]).astype(BF16)

    @pl.when(c == pl.num_programs(1) - 1)
    def _():
        for k in range(SSD_INNER // LANE):
            st_ref[0, k * LANE:(k + 1) * LANE, :] = s_ref[:, k * LANE:(k + 1) * LANE].T


def _pad_lanes(v, n=LANE):
    v = v.reshape(1, -1).astype(F32)
    return jnp.pad(v, ((0, 0), (0, n - v.shape[1])))


def ssd_prompt(proj, batch, seq, conv_w, conv_b, dt_bias, a_log, d_skip, norm_g):
    L = SSD_CHUNK
    nc = seq // L
    rowb = lambda b, c: b * nc + c
    full = lambda shape: pl.BlockSpec(shape, lambda b, c: (0,) * len(shape))
    y, st = pl.pallas_call(
        _ssd_prompt_kernel,
        grid=(batch, nc),
        in_specs=[
            pl.BlockSpec((L, SSD_INNER), lambda b, c: (rowb(b, c), 0)),
            pl.BlockSpec((L, SSD_INNER), lambda b, c: (rowb(b, c), 1)),
            pl.BlockSpec((L, 2 * SSD_GROUPS * SSD_N), lambda b, c: (rowb(b, c), 4)),
            pl.BlockSpec((L, LANE), lambda b, c: (rowb(b, c), (SSD_INNER + SSD_CONV_DIM) // LANE)),
            full((4, SSD_CONV_DIM)), full((1, SSD_CONV_DIM)), full((1, LANE)), full((1, LANE)),
            full((1, SSD_INNER)), full((1, SSD_INNER)),
        ],
        out_specs=[
            pl.BlockSpec((L, SSD_INNER), lambda b, c: (rowb(b, c), 0)),
            pl.BlockSpec((1, SSD_INNER, SSD_N), lambda b, c: (b, 0, 0)),
        ],
        out_shape=[
            jax.ShapeDtypeStruct((batch * seq, SSD_INNER), BF16),
            jax.ShapeDtypeStruct((batch, SSD_INNER, SSD_N), F32),
        ],
        scratch_shapes=[
            pltpu.VMEM((L + 8, SSD_CONV_DIM), F32),
            pltpu.VMEM((SSD_N, SSD_INNER), F32),
            pltpu.VMEM((L, SSD_INNER), F32),
        ],
        compiler_params=_cparams("parallel", "arbitrary"),
        name="ssd_prompt",
    )(proj, proj, proj, proj, conv_w, conv_b.reshape(1, -1), _pad_lanes(dt_bias), _pad_lanes(a_log),
      jnp.repeat(d_skip, SSD_P).reshape(1, -1), norm_g.reshape(1, -1))
    return y, st.reshape(batch, SSD_HEADS, SSD_P, SSD_N)


def _ssd_sample_pre_kernel(x_ref, bc_ref, dt_ref, prev_ref, cw_ref, cb_ref, dtb_ref, alog_ref,
                           xs_ref, b_ref, c_ref, xdt_ref, dec_ref):
    xnew = jnp.concatenate([x_ref[...], bc_ref[...]], axis=1)
    conv = cb_ref[...] + cw_ref[3:4, :] * xnew
    for j in range(3):
        conv = conv + cw_ref[j:j + 1, :] * prev_ref[j]
    act = _silu(conv)
    xs = act[:, 0:SSD_INNER]
    dt = _softplus(dt_ref[...] + dtb_ref[...])
    da = dt * (-jnp.exp(alog_ref[...]))
    expand = _head_expand(SSD_INNER)
    xs_ref[...] = xs
    b_ref[...] = act[:, SSD_INNER:SSD_INNER + SSD_GROUPS * SSD_N]
    c_ref[...] = act[:, SSD_INNER + SSD_GROUPS * SSD_N:SSD_CONV_DIM]
    xdt_ref[...] = xs * _dot(dt, expand, precision=HIGHEST)
    dec_ref[...] = jnp.exp(_dot(da, expand, precision=HIGHEST))


def _ssd_sample_state_kernel(s_ref, xdt_ref, dec_ref, b_ref, c_ref, so_ref, y_ref):
    eye = _eye(LANE)
    for k in range(SSD_INNER // LANE):
        g = k // (SSD_GW // LANE)
        lo = k * LANE
        xcol = _row_to_col(xdt_ref[0, :, lo:lo + LANE], eye)
        dcol = _row_to_col(dec_ref[0, :, lo:lo + LANE], eye)
        brow = b_ref[0, :, g * SSD_N:(g + 1) * SSD_N]
        crow = c_ref[0, :, g * SSD_N:(g + 1) * SSD_N]
        snew = dcol * s_ref[0, lo:lo + LANE, :] + xcol * brow
        so_ref[0, lo:lo + LANE, :] = snew
        ycol = jnp.sum(snew * crow, axis=-1, keepdims=True)
        y_ref[0, :, lo:lo + LANE] = _col_to_row(ycol, eye)


def _ssd_sample_post_kernel(y_ref, xs_ref, z_ref, dx_ref, ng_ref, o_ref):
    y = y_ref[...] + xs_ref[...] * dx_ref[...]
    yz = y * _silu(z_ref[...])
    for g in range(SSD_GROUPS):
        lo_f = g * SSD_GW
        o_ref[:, lo_f:lo_f + SSD_GW] = _rms(yz[:, lo_f:lo_f + SSD_GW], ng_ref[:, lo_f:lo_f + SSD_GW]).astype(BF16)


def ssd_sample(proj, conv_prev, state, conv_w, conv_b, dt_bias, a_log, d_skip, norm_g):
    nb = proj.shape[0]
    full = lambda shape: pl.BlockSpec(shape, lambda *_: (0,) * len(shape))
    row_spec = lambda width, blk: pl.BlockSpec((nb, width), lambda *_: (0, blk))
    sds = lambda width: jax.ShapeDtypeStruct((nb, width), F32)
    gn = SSD_GROUPS * SSD_N
    xs, bm, cm, xdt, dec = pl.pallas_call(
        _ssd_sample_pre_kernel,
        grid=(1,),
        in_specs=[
            row_spec(SSD_INNER, 1), row_spec(2 * gn, 4), row_spec(LANE, (SSD_INNER + SSD_CONV_DIM) // LANE),
            full((3, nb, SSD_CONV_DIM)), full((4, SSD_CONV_DIM)), full((1, SSD_CONV_DIM)), full((1, LANE)), full((1, LANE)),
        ],
        out_specs=[full((nb, SSD_INNER)), full((nb, gn)), full((nb, gn)), full((nb, SSD_INNER)), full((nb, SSD_INNER))],
        out_shape=[sds(SSD_INNER), sds(gn), sds(gn), sds(SSD_INNER), sds(SSD_INNER)],
        compiler_params=_cparams("arbitrary"),
        name="ssd_sample_pre",
    )(proj, proj, proj, jnp.swapaxes(conv_prev, 0, 1), conv_w, conv_b.reshape(1, -1), _pad_lanes(dt_bias), _pad_lanes(a_log))
    per_row = lambda width: pl.BlockSpec((1, 1, width), lambda b: (b, 0, 0))
    s_new, y = pl.pallas_call(
        _ssd_sample_state_kernel,
        grid=(nb,),
        in_specs=[
            pl.BlockSpec((1, SSD_INNER, SSD_N), lambda b: (b, 0, 0)),
            per_row(SSD_INNER), per_row(SSD_INNER), per_row(gn), per_row(gn),
        ],
        out_specs=[pl.BlockSpec((1, SSD_INNER, SSD_N), lambda b: (b, 0, 0)), per_row(SSD_INNER)],
        out_shape=[jax.ShapeDtypeStruct((nb, SSD_INNER, SSD_N), F32), jax.ShapeDtypeStruct((nb, 1, SSD_INNER), F32)],
        compiler_params=_cparams("parallel"),
        name="ssd_sample_state",
    )(state.reshape(nb, SSD_INNER, SSD_N), xdt.reshape(nb, 1, -1), dec.reshape(nb, 1, -1),
      bm.reshape(nb, 1, -1), cm.reshape(nb, 1, -1))
    y = y.reshape(nb, SSD_INNER)
    out = pl.pallas_call(
        _ssd_sample_post_kernel,
        grid=(1,),
        in_specs=[full((nb, SSD_INNER)), full((nb, SSD_INNER)), row_spec(SSD_INNER, 0), full((1, SSD_INNER)), full((1, SSD_INNER))],
        out_specs=full((nb, SSD_INNER)),
        out_shape=jax.ShapeDtypeStruct((nb, SSD_INNER), BF16),
        compiler_params=_cparams("arbitrary"),
        name="ssd_sample_post",
    )(y, xs, proj, jnp.repeat(d_skip, SSD_P).reshape(1, -1), norm_g.reshape(1, -1))
    return out, s_new.reshape(nb, SSD_HEADS, SSD_P, SSD_N)


def ssd_weights(w_in):
    n = _round_up(w_in.shape[1], COL_TILE)
    return _pad_cols(w_in, n).astype(BF16)


def ssd_layer(xp, xs, batch, seq, norm_g_mix, w_in, conv_w, conv_b, dt_bias, a_log, d_skip, norm_g, w_out,
              conv_state, ssm_state):
    w_in_b = ssd_weights(w_in)
    w_out_b = w_out.astype(BF16)
    args = (conv_w, conv_b, dt_bias, a_log, d_skip, norm_g)
    proj_p = norm_proj(xp, norm_g_mix, w_in_b)
    y_p, st_p = ssd_prompt(proj_p, batch, seq, *args)
    xp = proj_res(y_p, w_out_b, xp)
    raw_p = proj_p.reshape(batch, seq, -1)[:, seq - 3:, SSD_INNER:SSD_INNER + SSD_CONV_DIM]
    proj_s = norm_proj(xs, norm_g_mix, w_in_b)
    y_s, st_s = ssd_sample(proj_s, conv_state, ssm_state, *args)
    xs = proj_res(y_s, w_out_b, xs)
    raw_s = jnp.concatenate([conv_state[:, 1:], proj_s[:, None, SSD_INNER:SSD_INNER + SSD_CONV_DIM]], axis=1)
    return xp, xs, raw_p, st_p, raw_s, st_s


def ffn_weights(w_gate, w_up, w_down):
    wg = _pad_cols(w_gate, D_FF_PAD).astype(BF16)
    wu = _pad_cols(w_up, D_FF_PAD).astype(BF16)
    wd = jnp.pad(w_down, ((0, D_FF_PAD - D_FF), (0, 0))).astype(BF16)
    return wg, wu, wd


HEAD_DIM = 128
ATT_HEADS = 16
ATT_BLOCK = 128
REL_BUCKETS = 32
REL_MAX_DIST = 128


def rel_bucket(dist):
    exact = REL_BUCKETS // 2
    d = jnp.maximum(dist, 0)
    logd = jnp.log(jnp.maximum(d, 1).astype(F32) / exact)
    far = exact + (logd / math.log(REL_MAX_DIST / exact) * (REL_BUCKETS - exact)).astype(jnp.int32)
    return jnp.where(d < exact, d, jnp.minimum(far, REL_BUCKETS - 1))


def local_bias(table, n_prev, window, groups):
    width = (n_prev + 1) * ATT_BLOCK
    dist = n_prev * ATT_BLOCK + jnp.arange(ATT_BLOCK)[:, None] - jnp.arange(width)[None, :]
    bias = table.T[:, rel_bucket(dist)].astype(F32)
    bias = jnp.where((dist >= 0) & (dist <= window), bias, NEG)
    return bias.reshape(groups, (ATT_HEADS // groups) * ATT_BLOCK, width)


def _banded_kernel(q_ref, k_ref, v_ref, bias_ref, sink_ref, o_ref, *, groups, n_prev, use_sink):
    qi = pl.program_id(1)
    rep = ATT_HEADS // groups
    scale = HEAD_DIM ** -0.5
    for g in range(groups):
        qs = jnp.concatenate(
            [q_ref[:, (g * rep + r) * HEAD_DIM:(g * rep + r + 1) * HEAD_DIM] for r in range(rep)], axis=0)
        qs = (qs * scale).astype(BF16)
        logits = []
        vals = []
        for s in range(n_prev + 1):
            kb = qi - n_prev + s
            start = pl.multiple_of(jnp.maximum(kb, 0) * ATT_BLOCK, ATT_BLOCK)
            kblk = k_ref[pl.ds(start, ATT_BLOCK), g * HEAD_DIM:(g + 1) * HEAD_DIM].astype(BF16)
            vals.append(v_ref[pl.ds(start, ATT_BLOCK), g * HEAD_DIM:(g + 1) * HEAD_DIM].astype(BF16))
            l_s = _dot_nt(qs, kblk) + bias_ref[g, :, s * ATT_BLOCK:(s + 1) * ATT_BLOCK]
            logits.append(jnp.where(kb >= 0, l_s, NEG))
        m = functools.reduce(jnp.maximum, [jnp.max(l, axis=-1, keepdims=True) for l in logits])
        if use_sink:
            sink = sink_ref[g][:, 0:1]
            m = jnp.maximum(m, sink)
            denom = jnp.exp(sink - m)
        else:
            denom = jnp.zeros_like(m)
        acc = jnp.zeros((rep * ATT_BLOCK, HEAD_DIM), F32)
        for l_s, vblk in zip(logits, vals):
            p = jnp.exp(l_s - m)
            denom = denom + jnp.sum(p, axis=-1, keepdims=True)
            acc = acc + _dot(p.astype(BF16), vblk)
        out = acc / denom
        for r in range(rep):
            h = g * rep + r
            o_ref[:, h * HEAD_DIM:(h + 1) * HEAD_DIM] = out[r * ATT_BLOCK:(r + 1) * ATT_BLOCK, :].astype(o_ref.dtype)


def banded_attention(proj, batch, seq, q_blk, k_blk, v_blk, groups, n_prev, bias, sink_rows, out_dtype):
    nq = seq // ATT_BLOCK
    kvw = groups * HEAD_DIM
    rep = ATT_HEADS // groups
    use_sink = sink_rows is not None
    if sink_rows is None:
        sink_rows = jnp.zeros((groups, rep * ATT_BLOCK, LANE), F32)
    return pl.pallas_call(
        functools.partial(_banded_kernel, groups=groups, n_prev=n_prev, use_sink=use_sink),
        grid=(batch, nq),
        in_specs=[
            pl.BlockSpec((ATT_BLOCK, ATT_HEADS * HEAD_DIM), lambda b, i: (b * nq + i, q_blk)),
            pl.BlockSpec((seq, kvw), lambda b, i: (b, k_blk)),
            pl.BlockSpec((seq, kvw), lambda b, i: (b, v_blk)),
            pl.BlockSpec(bias.shape, lambda b, i: (0, 0, 0)),
            pl.BlockSpec(sink_rows.shape, lambda b, i: (0, 0, 0)),
        ],
        out_specs=pl.BlockSpec((ATT_BLOCK, ATT_HEADS * HEAD_DIM), lambda b, i: (b * nq + i, 0)),
        out_shape=jax.ShapeDtypeStruct((batch * seq, ATT_HEADS * HEAD_DIM), out_dtype),
        compiler_params=_cparams("parallel", "arbitrary"),
        name="banded_attention",
    )(proj, proj, proj, bias, sink_rows)


def _decode_kernel(q_ref, kn_ref, vn_ref, kb_ref, vb_ref, bias_ref, bnew_ref, sink_ref, o_ref, *, groups, use_sink):
    rep = ATT_HEADS // groups
    scale = HEAD_DIM ** -0.5
    for g in range(groups):
        qg = jnp.concatenate(
            [q_ref[0, :, (g * rep + r) * HEAD_DIM:(g * rep + r + 1) * HEAD_DIM] for r in range(rep)], axis=0)
        qg = qg * scale
        kbuf = kb_ref[0, :, g * HEAD_DIM:(g + 1) * HEAD_DIM].astype(BF16)
        vbuf = vb_ref[0, :, g * HEAD_DIM:(g + 1) * HEAD_DIM].astype(BF16)
        knew = kn_ref[0, :, g * HEAD_DIM:(g + 1) * HEAD_DIM]
        vnew = vn_ref[0, :, g * HEAD_DIM:(g + 1) * HEAD_DIM]
        l_c = _dot_nt(qg.astype(BF16), kbuf) + bias_ref[g * rep:(g + 1) * rep, :]
        l_n = jnp.sum(qg * knew, axis=-1, keepdims=True) + bnew_ref[g * rep:(g + 1) * rep, 0:1]
        m = jnp.maximum(jnp.max(l_c, axis=-1, keepdims=True), l_n)
        if use_sink:
            sink = sink_ref[g * rep:(g + 1) * rep, 0:1]
            m = jnp.maximum(m, sink)
            denom = jnp.exp(sink - m)
        else:
            denom = jnp.zeros_like(m)
        p_c = jnp.exp(l_c - m)
        p_n = jnp.exp(l_n - m)
        denom = denom + jnp.sum(p_c, axis=-1, keepdims=True) + p_n
        out = (_dot(p_c.astype(BF16), vbuf) + p_n * vnew) / denom
        for r in range(rep):
            h = g * rep + r
            o_ref[0, :, h * HEAD_DIM:(h + 1) * HEAD_DIM] = out[r:r + 1, :].astype(o_ref.dtype)


def decode_attention(proj, q_blk, k_blk, v_blk, k_buf, v_buf, groups, table, sink, out_dtype):
    nb, wn = k_buf.shape[0], k_buf.shape[1]
    kvw = groups * HEAD_DIM
    bias = table.T[:, rel_bucket(wn - jnp.arange(wn))].astype(F32)
    bnew = jnp.broadcast_to(table[0][:, None].astype(F32), (ATT_HEADS, LANE))
    use_sink = sink is not None
    sink_rows = jnp.broadcast_to((sink if use_sink else jnp.zeros((ATT_HEADS,), F32))[:, None].astype(F32), (ATT_HEADS, LANE))
    full = lambda shape: pl.BlockSpec(shape, lambda b: (0,) * len(shape))
    proj3 = proj.reshape(nb, 1, -1)
    out = pl.pallas_call(
        functools.partial(_decode_kernel, groups=groups, use_sink=use_sink),
        grid=(nb,),
        in_specs=[
            pl.BlockSpec((1, 1, ATT_HEADS * HEAD_DIM), lambda b: (b, 0, q_blk)),
            pl.BlockSpec((1, 1, kvw), lambda b: (b, 0, k_blk)),
            pl.BlockSpec((1, 1, kvw), lambda b: (b, 0, v_blk)),
            pl.BlockSpec((1, wn, kvw), lambda b: (b, 0, 0)),
            pl.BlockSpec((1, wn, kvw), lambda b: (b, 0, 0)),
            full((ATT_HEADS, wn)), full((ATT_HEADS, LANE)), full((ATT_HEADS, LANE)),
        ],
        out_specs=pl.BlockSpec((1, 1, ATT_HEADS * HEAD_DIM), lambda b: (b, 0, 0)),
        out_shape=jax.ShapeDtypeStruct((nb, 1, ATT_HEADS * HEAD_DIM), out_dtype),
        compiler_params=_cparams("parallel"),
        name="decode_attention",
    )(proj3, proj3, proj3, k_buf.reshape(nb, wn, kvw), v_buf.reshape(nb, wn, kvw), bias, bnew, sink_rows)
    return out.reshape(nb, ATT_HEADS * HEAD_DIM)


SWA_KV = 4
SWA_WINDOW = 128


def swa_layer(xp, xs, batch, seq, norm_g_mix, table, w_qkv, sink, w_o, k_buf, v_buf):
    w_qkv_b = w_qkv.astype(BF16)
    w_o_b = w_o.astype(BF16)
    kvw = SWA_KV * HEAD_DIM
    qd = ATT_HEADS * HEAD_DIM
    rep = ATT_HEADS // SWA_KV
    n_prev = -(-SWA_WINDOW // ATT_BLOCK)
    bias = local_bias(table, n_prev, SWA_WINDOW, SWA_KV)
    sink_rows = jnp.broadcast_to(jnp.repeat(sink.astype(F32), ATT_BLOCK).reshape(SWA_KV, rep * ATT_BLOCK, 1),
                                 (SWA_KV, rep * ATT_BLOCK, LANE))
    proj_p = norm_proj(xp, norm_g_mix, w_qkv_b)
    o_p = banded_attention(proj_p, batch, seq, 0, qd // kvw, qd // kvw + 1, SWA_KV, n_prev, bias, sink_rows, BF16)
    xp = proj_res(o_p, w_o_b, xp)
    buf = min(SWA_WINDOW, seq)
    kv_p = proj_p.reshape(batch, seq, -1)[:, seq - buf:, qd:]
    k_p = kv_p[..., :kvw].reshape(batch, buf, SWA_KV, HEAD_DIM)
    v_p = kv_p[..., kvw:2 * kvw].reshape(batch, buf, SWA_KV, HEAD_DIM)
    proj_s = norm_proj(xs, norm_g_mix, w_qkv_b)
    o_s = decode_attention(proj_s, 0, qd // kvw, qd // kvw + 1, k_buf, v_buf, SWA_KV, table, sink, BF16)
    xs = proj_res(o_s, w_o_b, xs)
    nb = xs.shape[0]
    k_s = jnp.concatenate([k_buf[:, 1:], proj_s[:, None, qd:qd + kvw].reshape(nb, 1, SWA_KV, HEAD_DIM)], axis=1)
    v_s = jnp.concatenate([v_buf[:, 1:], proj_s[:, None, qd + kvw:qd + 2 * kvw].reshape(nb, 1, SWA_KV, HEAD_DIM)], axis=1)
    return xp, xs, k_p, v_p, k_s, v_s


GDN_K_HEADS = 16
GDN_V_HEADS = 32
GDN_D = 128
GDN_KEY = GDN_K_HEADS * GDN_D
GDN_VAL = GDN_V_HEADS * GDN_D
GDN_CONV_DIM = 2 * GDN_KEY + GDN_VAL
GDN_CHUNK = 64
GDN_BETA_BLK = (GDN_CONV_DIM + GDN_VAL) // LANE


def gdn_weights(w_in):
    c2 = GDN_CONV_DIM + GDN_VAL
    parts = [w_in[:, :c2], _pad_cols(w_in[:, c2:c2 + GDN_V_HEADS], LANE), _pad_cols(w_in[:, c2 + GDN_V_HEADS:], LANE)]
    w = jnp.concatenate(parts, axis=1)
    return _pad_cols(w, _round_up(w.shape[1], COL_TILE)).astype(BF16)


def _split_bf16(a):
    hi = a.astype(BF16)
    return hi, (a - hi.astype(F32)).astype(BF16)


def _dot3(a, b):
    ah, al = _split_bf16(a)
    bh, bl = _split_bf16(b)
    return _dot(ah, bh) + (_dot(ah, bl) + _dot(al, bh))


def _unit_lower_inverse(m):
    n = m.shape[0]
    p = -m
    t = jnp.where(_eye(n), 1.0, 0.0) + p
    p = _dot3(p, p)
    k = 2
    while k < n:
        last = 2 * k >= n
        if last:
            t = t + _dot3(t, p)
        else:
            both = _dot3(jnp.concatenate([t, p], axis=0), p)
            t = t + both[:n]
            p = both[n:]
        k *= 2
    return t


def _l2n(x):
    return x * lax.rsqrt(jnp.sum(x * x, axis=-1, keepdims=True) + EPS)


def _gdn_prompt_kernel(qk_ref, v_ref, z_ref, beta_ref, a_ref, cw_ref, dtb_ref, alog_ref, ng_ref,
                       o_ref, st_ref, xpad_ref, s_ref):
    c = pl.program_id(1)
    L = GDN_CHUNK

    @pl.when(c == 0)
    def _():
        xpad_ref[0:8, :] = jnp.zeros((8, GDN_CONV_DIM), F32)
        s_ref[...] = jnp.zeros_like(s_ref)

    xpad_ref[8:8 + L, 0:2 * GDN_KEY] = qk_ref[...]
    xpad_ref[8:8 + L, 2 * GDN_KEY:GDN_CONV_DIM] = v_ref[...]
    conv = cw_ref[3:4, :] * xpad_ref[8:8 + L, :]
    for j in range(3):
        conv = conv + cw_ref[j:j + 1, :] * xpad_ref[pl.ds(5 + j, L), :]
    xpad_ref[0:8, :] = xpad_ref[L:L + 8, :]
    act = _silu(conv)

    beta = _sigmoid(beta_ref[...])
    g = -jnp.exp(alog_ref[...]) * _softplus(a_ref[...] + dtb_ref[...])
    row = lax.broadcasted_iota(jnp.int32, (L, L), 0)
    col = lax.broadcasted_iota(jnp.int32, (L, L), 1)
    incl = row >= col
    strict = row > col
    gc = _dot(jnp.where(incl, 1.0, 0.0).astype(F32), g, precision=HIGHEST)
    gc_t = gc.T
    eg = jnp.exp(gc)
    g_last = gc[L - 1:L, :]
    k_scale = jnp.exp(g_last - gc)
    s_scale = jnp.exp(g_last)
    ng = ng_ref[...]

    for kh in range(GDN_K_HEADS):
        qn = _l2n(act[:, kh * GDN_D:(kh + 1) * GDN_D]) * GDN_D ** -0.5
        kn = _l2n(act[:, GDN_KEY + kh * GDN_D:GDN_KEY + (kh + 1) * GDN_D])
        kn_b = kn.astype(BF16)
        prods = _dot_nt(jnp.concatenate([kn_b, qn.astype(BF16)], axis=0), kn_b)
        kk = prods[:L]
        qk = prods[L:]
        for vh in (2 * kh, 2 * kh + 1):
            v = act[:, 2 * GDN_KEY + vh * GDN_D:2 * GDN_KEY + (vh + 1) * GDN_D]
            bcol = beta[:, vh:vh + 1]
            egcol = eg[:, vh:vh + 1]
            decay = jnp.exp(jnp.where(incl, gc[:, vh:vh + 1] - gc_t[vh:vh + 1, :], NEG))
            t_inv = _unit_lower_inverse(jnp.where(strict, kk * bcol * decay, 0.0))
            rhs = jnp.concatenate([v * bcol, kn * (bcol * egcol)], axis=1).astype(BF16)
            uw = _dot(t_inv.astype(BF16), rhs)
            s_old = s_ref[vh]
            ws_qs = _dot(jnp.concatenate([uw[:, GDN_D:], qn * egcol], axis=0).astype(BF16), s_old.astype(BF16))
            v_new = uw[:, :GDN_D] - ws_qs[:L]
            v_new_b = v_new.astype(BF16)
            o = ws_qs[L:] + _dot((qk * decay).astype(BF16), v_new_b)
            k_dec = (kn * k_scale[:, vh:vh + 1]).astype(BF16)
            s_ref[vh] = s_old * s_scale[:, vh:vh + 1] + _dot_tn(k_dec, v_new_b)
            zz = z_ref[:, vh * GDN_D:(vh + 1) * GDN_D]
            o_ref[:, vh * GDN_D:(vh + 1) * GDN_D] = (_rms(o, ng) * _silu(zz)).astype(BF16)

    @pl.when(c == pl.num_programs(1) - 1)
    def _():
        st_ref[0] = s_ref[...]


def gdn_prompt(proj, batch, seq, conv_w, dt_bias, a_log, norm_g):
    L = GDN_CHUNK
    nc = seq // L
    rowb = lambda b, c: b * nc + c
    full = lambda shape: pl.BlockSpec(shape, lambda b, c: (0,) * len(shape))
    return pl.pallas_call(
        _gdn_prompt_kernel,
        grid=(batch, nc),
        in_specs=[
            pl.BlockSpec((L, 2 * GDN_KEY), lambda b, c: (rowb(b, c), 0)),
            pl.BlockSpec((L, GDN_VAL), lambda b, c: (rowb(b, c), 1)),
            pl.BlockSpec((L, GDN_VAL), lambda b, c: (rowb(b, c), 2)),
            pl.BlockSpec((L, LANE), lambda b, c: (rowb(b, c), GDN_BETA_BLK)),
            pl.BlockSpec((L, LANE), lambda b, c: (rowb(b, c), GDN_BETA_BLK + 1)),
            full((4, GDN_CONV_DIM)), full((1, LANE)), full((1, LANE)), full((1, GDN_D)),
        ],
        out_specs=[
            pl.BlockSpec((L, GDN_VAL), lambda b, c: (rowb(b, c), 0)),
            pl.BlockSpec((1, GDN_V_HEADS, GDN_D, GDN_D), lambda b, c: (b, 0, 0, 0)),
        ],
        out_shape=[
            jax.ShapeDtypeStruct((batch * seq, GDN_VAL), BF16),
            jax.ShapeDtypeStruct((batch, GDN_V_HEADS, GDN_D, GDN_D), F32),
        ],
        scratch_shapes=[pltpu.VMEM((L + 8, GDN_CONV_DIM), F32), pltpu.VMEM((GDN_V_HEADS, GDN_D, GDN_D), F32)],
        compiler_params=_cparams("parallel", "arbitrary"),
        name="gdn_prompt",
    )(proj, proj, proj, proj, proj, conv_w, _pad_lanes(dt_bias), _pad_lanes(a_log), norm_g.reshape(1, -1))


def _gdn_sample_pre_kernel(qk_ref, v_ref, beta_ref, a_ref, prev_ref, cw_ref, dtb_ref, alog_ref,
                           q_ref, k_ref, vo_ref, bo_ref, eg_ref):
    xnew = jnp.concatenate([qk_ref[...], v_ref[...]], axis=1)
    conv = cw_ref[3:4, :] * xnew
    for j in range(3):
        conv = conv + cw_ref[j:j + 1, :] * prev_ref[j]
    act = _silu(conv)
    for kh in range(GDN_K_HEADS):
        q_ref[:, kh * GDN_D:(kh + 1) * GDN_D] = _l2n(act[:, kh * GDN_D:(kh + 1) * GDN_D]) * GDN_D ** -0.5
        k_ref[:, kh * GDN_D:(kh + 1) * GDN_D] = _l2n(act[:, GDN_KEY + kh * GDN_D:GDN_KEY + (kh + 1) * GDN_D])
    vo_ref[...] = act[:, 2 * GDN_KEY:]
    bo_ref[...] = _sigmoid(beta_ref[...])
    eg_ref[...] = jnp.exp(-jnp.exp(alog_ref[...]) * _softplus(a_ref[...] + dtb_ref[...]))


def _gdn_sample_state_kernel(s_ref, q_ref, k_ref, v_ref, z_ref, beta_ref, eg_ref, ng_ref, so_ref, o_ref):
    eye = _eye(GDN_D)
    ng = ng_ref[...]
    for kh in range(GDN_K_HEADS):
        qrow = q_ref[0, :, kh * GDN_D:(kh + 1) * GDN_D]
        krow = k_ref[0, :, kh * GDN_D:(kh + 1) * GDN_D]
        qcol = _row_to_col(qrow, eye)
        kcol = _row_to_col(krow, eye)
        qk = jnp.sum(qrow * krow, axis=-1, keepdims=True)
        for vh in (2 * kh, 2 * kh + 1):
            s_old = s_ref[0, vh]
            beta = beta_ref[0, :, vh:vh + 1]
            eg = eg_ref[0, :, vh:vh + 1]
            v = v_ref[0, :, vh * GDN_D:(vh + 1) * GDN_D]
            ks = jnp.sum(kcol * s_old, axis=0, keepdims=True)
            qs = jnp.sum(qcol * s_old, axis=0, keepdims=True)
            v_new = beta * v - (beta * eg) * ks
            o = eg * qs + qk * v_new
            so_ref[0, vh] = s_old * eg + kcol * v_new
            zz = z_ref[0, :, vh * GDN_D:(vh + 1) * GDN_D]
            o_ref[0, :, vh * GDN_D:(vh + 1) * GDN_D] = (_rms(o, ng) * _silu(zz)).astype(BF16)


def gdn_sample(proj, conv_prev, state, conv_w, dt_bias, a_log, norm_g):
    nb = proj.shape[0]
    full = lambda shape: pl.BlockSpec(shape, lambda *_: (0,) * len(shape))
    row_spec = lambda width, blk: pl.BlockSpec((nb, width), lambda *_: (0, blk))
    sds = lambda width: jax.ShapeDtypeStruct((nb, width), F32)
    qn, kn, v, beta, eg = pl.pallas_call(
        _gdn_sample_pre_kernel,
        grid=(1,),
        in_specs=[
            row_spec(2 * GDN_KEY, 0), row_spec(GDN_VAL, 1), row_spec(LANE, GDN_BETA_BLK), row_spec(LANE, GDN_BETA_BLK + 1),
            full((3, nb, GDN_CONV_DIM)), full((4, GDN_CONV_DIM)), full((1, LANE)), full((1, LANE)),
        ],
        out_specs=[full((nb, GDN_KEY)), full((nb, GDN_KEY)), full((nb, GDN_VAL)), full((nb, LANE)), full((nb, LANE))],
        out_shape=[sds(GDN_KEY), sds(GDN_KEY), sds(GDN_VAL), sds(LANE), sds(LANE)],
        compiler_params=_cparams("arbitrary"),
        name="gdn_sample_pre",
    )(proj, proj, proj, proj, jnp.swapaxes(conv_prev, 0, 1), conv_w, _pad_lanes(dt_bias), _pad_lanes(a_log))
    per_row = lambda width, blk=0: pl.BlockSpec((1, 1, width), lambda b: (b, 0, blk))
    st_spec = pl.BlockSpec((1, GDN_V_HEADS, GDN_D, GDN_D), lambda b: (b, 0, 0, 0))
    r3 = lambda a: a.reshape(nb, 1, -1)
    s_new, o = pl.pallas_call(
        _gdn_sample_state_kernel,
        grid=(nb,),
        in_specs=[st_spec, per_row(GDN_KEY), per_row(GDN_KEY), per_row(GDN_VAL), per_row(GDN_VAL, 2),
                  per_row(LANE), per_row(LANE), pl.BlockSpec((1, GDN_D), lambda b: (0, 0))],
        out_specs=[st_spec, per_row(GDN_VAL)],
        out_shape=[jax.ShapeDtypeStruct(state.shape, F32), jax.ShapeDtypeStruct((nb, 1, GDN_VAL), BF16)],
        compiler_params=_cparams("parallel"),
        name="gdn_sample_state",
    )(state, r3(qn), r3(kn), r3(v), r3(proj), r3(beta), r3(eg), norm_g.reshape(1, -1))
    return o.reshape(nb, GDN_VAL), s_new


def gdn_layer(xp, xs, batch, seq, norm_g_mix, w_in, conv_w, dt_bias, a_log, norm_g, w_out, conv_state, state):
    w_in_b = gdn_weights(w_in)
    w_out_b = w_out.astype(BF16)
    args = (conv_w, dt_bias, a_log, norm_g)
    proj_p = norm_proj(xp, norm_g_mix, w_in_b)
    o_p, st_p = gdn_prompt(proj_p, batch, seq, *args)
    xp = proj_res(o_p, w_out_b, xp)
    raw_p = proj_p.reshape(batch, seq, -1)[:, seq - 3:, :GDN_CONV_DIM]
    proj_s = norm_proj(xs, norm_g_mix, w_in_b)
    o_s, st_s = gdn_sample(proj_s, conv_state, state, *args)
    xs = proj_res(o_s, w_out_b, xs)
    raw_s = jnp.concatenate([conv_state[:, 1:], proj_s[:, None, :GDN_CONV_DIM]], axis=1)
    return xp, xs, raw_p, st_p, raw_s, st_s


NSA_KV = 2
NSA_BLOCK = 64
NSA_TOPN = 16
NSA_WINDOW = 512
NSA_HID = 256
NSA_FORCE = 1e4
NSA_REP = ATT_HEADS // NSA_KV
NSA_KVW = NSA_KV * HEAD_DIM
NSA_QD = ATT_HEADS * HEAD_DIM
NSA_GATE_BLK = (NSA_QD + 6 * NSA_KVW) // LANE
PAGE_SIZE = 128
PAGES_PER_STEP = 64


def _compress_block_rows(x_refs, pos_ref, w1_ref, w2_ref, o_ref):
    for g, x_ref in enumerate(x_refs):
        nblk = x_ref.shape[0] // NSA_BLOCK
        acc = jnp.zeros((nblk, NSA_HID), F32)
        for t in range(NSA_BLOCK):
            lhs = x_ref[pl.ds(t, nblk, stride=NSA_BLOCK), :] + pos_ref[t:t + 1, :]
            acc = acc + _dot(lhs.astype(BF16), w1_ref[t * HEAD_DIM:(t + 1) * HEAD_DIM, :])
        o_ref[:, g * HEAD_DIM:(g + 1) * HEAD_DIM] = _dot(_silu(acc).astype(BF16), w2_ref[...])


def _compress_rows_kernel(x0_ref, x1_ref, pos_ref, w1_ref, w2_ref, o_ref):
    _compress_block_rows((x0_ref, x1_ref), pos_ref, w1_ref, w2_ref, o_ref)


def compress_rows(proj, col_blk, pos, w1, w2):
    m = proj.shape[0]
    rows = min(m, 8192)
    full = lambda shape: pl.BlockSpec(shape, lambda i: (0,) * len(shape))
    head = lambda g: pl.BlockSpec((rows, HEAD_DIM), lambda i: (i, col_blk * NSA_KV + g))
    return pl.pallas_call(
        _compress_rows_kernel,
        grid=(m // rows,),
        in_specs=[head(0), head(1), full(pos.shape), full(w1.shape), full(w2.shape)],
        out_specs=pl.BlockSpec((rows // NSA_BLOCK, NSA_KVW), lambda i: (i, 0)),
        out_shape=jax.ShapeDtypeStruct((m // NSA_BLOCK, NSA_KVW), F32),
        compiler_params=_cparams("parallel"),
        name="compress_rows",
    )(proj, proj, pos, w1.astype(BF16), w2.astype(BF16))


def _compress_paged_kernel(pt_ref, pool_ref, pos_ref, w1_ref, w2_ref, o_ref, buf0_ref, buf1_ref, sem):
    i = pl.program_id(0)
    bufs = (buf0_ref, buf1_ref)

    def page_copy(j, g):
        page = pt_ref[i * PAGES_PER_STEP + j]
        return pltpu.make_async_copy(pool_ref.at[page, :, pl.ds(g * HEAD_DIM, HEAD_DIM)],
                                     bufs[g].at[pl.ds(j * PAGE_SIZE, PAGE_SIZE), :], sem.at[g])

    for j in range(PAGES_PER_STEP):
        for g in range(NSA_KV):
            page_copy(j, g).start()
    for j in range(PAGES_PER_STEP):
        for g in range(NSA_KV):
            page_copy(j, g).wait()
    _compress_block_rows(bufs, pos_ref, w1_ref, w2_ref, o_ref)


def compress_paged(pool, page_table, pos, w1, w2):
    n_pool = pool.shape[0]
    pt = page_table.reshape(-1)
    steps = pt.shape[0] // PAGES_PER_STEP
    blocks = PAGES_PER_STEP * (PAGE_SIZE // NSA_BLOCK)
    full = lambda shape: pl.BlockSpec(shape, lambda i, pt: (0,) * len(shape))
    return pl.pallas_call(
        _compress_paged_kernel,
        grid_spec=pltpu.PrefetchScalarGridSpec(
            num_scalar_prefetch=1,
            grid=(steps,),
            in_specs=[pl.BlockSpec(memory_space=pl.ANY), full(pos.shape), full(w1.shape), full(w2.shape)],
            out_specs=pl.BlockSpec((blocks, NSA_KVW), lambda i, pt: (i, 0)),
            scratch_shapes=[pltpu.VMEM((PAGES_PER_STEP * PAGE_SIZE, HEAD_DIM), F32),
                            pltpu.VMEM((PAGES_PER_STEP * PAGE_SIZE, HEAD_DIM), F32),
                            pltpu.SemaphoreType.DMA((NSA_KV,))],
        ),
        out_shape=jax.ShapeDtypeStruct((steps * blocks, NSA_KVW), F32),
        compiler_params=_cparams("arbitrary"),
        name="compress_paged",
    )(pt, pool.reshape(n_pool, PAGE_SIZE, NSA_KVW), pos, w1.astype(BF16), w2.astype(BF16))


def _top_n(score, lane, n):
    sel = jnp.zeros(score.shape, F32)
    for _ in range(n):
        m = jnp.max(score, axis=-1, keepdims=True)
        j = jnp.min(jnp.where(score == m, lane, 1 << 30), axis=-1, keepdims=True)
        pick = lane == j
        sel = jnp.where(pick, 1.0, sel)
        score = jnp.where(pick, -jnp.inf, score)
    return sel


def _nsa_prompt_kernel(q_ref, win_ref, gate_ref, kc_ref, vc_ref, ks_ref, vs_ref, bc_ref, bl_ref, far_ref, o_ref):
    qi = pl.program_id(1)
    seq = ks_ref.shape[0]
    n_cmp = kc_ref.shape[0]
    scale = HEAD_DIM ** -0.5
    gates = _sigmoid(gate_ref[...])
    qpos = qi * ATT_BLOCK + lax.broadcasted_iota(jnp.int32, (ATT_BLOCK, n_cmp), 0)
    blk = lax.broadcasted_iota(jnp.int32, (ATT_BLOCK, n_cmp), 1)
    cur = lax.shift_right_logical(qpos, 6)
    valid = blk <= cur
    forced = valid & ((blk == 0) | (blk >= cur - 1))
    er = lax.broadcasted_iota(jnp.int32, (n_cmp, seq), 0)
    ec = lax.broadcasted_iota(jnp.int32, (n_cmp, seq), 1)
    expand = jnp.where(lax.shift_right_logical(ec, 6) == er, 1.0, 0.0).astype(BF16)
    for g in range(NSA_KV):
        kcg = kc_ref[:, g * HEAD_DIM:(g + 1) * HEAD_DIM].astype(BF16)
        vcg = vc_ref[:, g * HEAD_DIM:(g + 1) * HEAD_DIM].astype(BF16)
        qh = [(q_ref[:, (g * NSA_REP + r) * HEAD_DIM:(g * NSA_REP + r + 1) * HEAD_DIM] * scale).astype(BF16)
              for r in range(NSA_REP)]
        bias_c = bc_ref[g, 0]
        ok_c = bias_c > 0.5 * NEG
        lc = _dot_nt(jnp.concatenate(qh, axis=0), kcg) + bias_c
        p = jnp.where(ok_c, jnp.exp(lc - jnp.max(lc, axis=-1, keepdims=True)), 0.0)
        p = p / jnp.maximum(jnp.sum(p, axis=-1, keepdims=True), 1e-30)
        o_cmp = _dot(p.astype(BF16), vcg)
        p_slc = functools.reduce(lambda a, b: a + b, [p[r * ATT_BLOCK:(r + 1) * ATT_BLOCK] for r in range(NSA_REP)])
        score = jnp.where(forced, NSA_FORCE, jnp.where(valid, p_slc, -1.0))
        sel = jnp.where(valid, _top_n(score, blk, NSA_TOPN), 0.0)
        sel_keys = _dot(sel.astype(BF16), expand) > 0.5
        ksg = ks_ref[:, g * HEAD_DIM:(g + 1) * HEAD_DIM].astype(BF16)
        vsg = vs_ref[:, g * HEAD_DIM:(g + 1) * HEAD_DIM].astype(BF16)
        for r in range(NSA_REP):
            h = g * NSA_REP + r
            pieces = []
            for c in range(seq // ATT_BLOCK):
                delta = qi - c
                far = jnp.where(delta >= 2, far_ref[h:h + 1, 0:1], NEG)
                pieces.append(jnp.where(delta == 0, bl_ref[h, :, ATT_BLOCK:2 * ATT_BLOCK],
                                        jnp.where(delta == 1, bl_ref[h, :, 0:ATT_BLOCK], far)))
            bias = jnp.concatenate(pieces, axis=1)
            ok = sel_keys & (bias > 0.5 * NEG)
            ls = jnp.where(ok, _dot_nt(qh[r], ksg) + bias, NEG)
            ps = jnp.where(ok, jnp.exp(ls - jnp.max(ls, axis=-1, keepdims=True)), 0.0)
            den = jnp.maximum(jnp.sum(ps, axis=-1, keepdims=True), 1e-30)
            o_sel = _dot(ps.astype(BF16), vsg) / den
            out = (gates[:, 3 * h:3 * h + 1] * o_cmp[r * ATT_BLOCK:(r + 1) * ATT_BLOCK]
                   + gates[:, 3 * h + 1:3 * h + 2] * o_sel
                   + gates[:, 3 * h + 2:3 * h + 3] * win_ref[:, h * HEAD_DIM:(h + 1) * HEAD_DIM])
            o_ref[:, h * HEAD_DIM:(h + 1) * HEAD_DIM] = out.astype(BF16)


def nsa_prompt_attention(proj, o_win, kc_blk, vc_blk, table, batch, seq):
    nq = seq // ATT_BLOCK
    n_cmp = seq // NSA_BLOCK
    assert n_cmp >= NSA_TOPN
    dist = jnp.arange(seq)[:, None] - (jnp.arange(n_cmp) * NSA_BLOCK + NSA_BLOCK - 1)[None, :]
    bias_c = jnp.where(dist >= 0, table.T[:, rel_bucket(dist)].astype(F32), NEG)
    bias_c = bias_c.reshape(NSA_KV, NSA_REP, nq, ATT_BLOCK, n_cmp).transpose(0, 2, 1, 3, 4)
    bias_c = bias_c.reshape(NSA_KV, nq, NSA_REP * ATT_BLOCK, n_cmp)
    bias_l = local_bias(table, 1, 1 << 30, ATT_HEADS).reshape(ATT_HEADS, ATT_BLOCK, 2 * ATT_BLOCK)
    far = jnp.broadcast_to(table[REL_BUCKETS - 1][:, None].astype(F32), (ATT_HEADS, LANE))
    full = lambda shape: pl.BlockSpec(shape, lambda b, i: (0,) * len(shape))
    return pl.pallas_call(
        _nsa_prompt_kernel,
        grid=(batch, nq),
        in_specs=[
            pl.BlockSpec((ATT_BLOCK, NSA_QD), lambda b, i: (b * nq + i, 0)),
            pl.BlockSpec((ATT_BLOCK, NSA_QD), lambda b, i: (b * nq + i, 0)),
            pl.BlockSpec((ATT_BLOCK, LANE), lambda b, i: (b * nq + i, NSA_GATE_BLK)),
            pl.BlockSpec((n_cmp, NSA_KVW), lambda b, i: (b, 0)),
            pl.BlockSpec((n_cmp, NSA_KVW), lambda b, i: (b, 0)),
            pl.BlockSpec((seq, NSA_KVW), lambda b, i: (b, NSA_QD // NSA_KVW + 2)),
            pl.BlockSpec((seq, NSA_KVW), lambda b, i: (b, NSA_QD // NSA_KVW + 3)),
            pl.BlockSpec((NSA_KV, 1, NSA_REP * ATT_BLOCK, n_cmp), lambda b, i: (0, i, 0, 0)),
            full(bias_l.shape), full(far.shape),
        ],
        out_specs=pl.BlockSpec((ATT_BLOCK, NSA_QD), lambda b, i: (b * nq + i, 0)),
        out_shape=jax.ShapeDtypeStruct((batch * seq, NSA_QD), BF16),
        compiler_params=_cparams("parallel", "arbitrary"),
        name="nsa_prompt_attention",
    )(proj, o_win, proj, kc_blk, vc_blk, proj, proj, bias_c, bias_l, far)


def _nsa_sample_cmp_kernel(q_ref, kc_ref, vc_ref, bias_ref, o_ref, idx_ref, *, cur, n_sel):
    n_cmp = kc_ref.shape[1]
    width = _round_up(n_sel, LANE)
    scale = HEAD_DIM ** -0.5
    lane = lax.broadcasted_iota(jnp.int32, (1, width), 1)
    lane_out = lax.broadcasted_iota(jnp.int32, (1, LANE), 1)
    valid = lane <= cur
    forced = valid & ((lane == 0) | (lane >= cur - 1))
    for g in range(NSA_KV):
        qg = jnp.concatenate([q_ref[0, :, (g * NSA_REP + r) * HEAD_DIM:(g * NSA_REP + r + 1) * HEAD_DIM]
                              for r in range(NSA_REP)], axis=0) * scale
        bias = bias_ref[g * NSA_REP:(g + 1) * NSA_REP, :]
        ok = bias > 0.5 * NEG
        lc = _dot_nt(qg.astype(BF16), kc_ref[0, :, g * HEAD_DIM:(g + 1) * HEAD_DIM].astype(BF16)) + bias
        p = jnp.where(ok, jnp.exp(lc - jnp.max(lc, axis=-1, keepdims=True)), 0.0)
        p = p / jnp.maximum(jnp.sum(p, axis=-1, keepdims=True), 1e-30)
        o_cmp = _dot(p.astype(BF16), vc_ref[0, :, g * HEAD_DIM:(g + 1) * HEAD_DIM].astype(BF16))
        for r in range(NSA_REP):
            h = g * NSA_REP + r
            o_ref[0, :, h * HEAD_DIM:(h + 1) * HEAD_DIM] = o_cmp[r:r + 1, :]
        p_slc = jnp.sum(p, axis=0, keepdims=True)
        score = jnp.concatenate([p_slc, jnp.zeros((1, width - n_cmp), F32)], axis=1)
        score = jnp.where(forced, NSA_FORCE, jnp.where(valid, score, -1.0))
        score = jnp.where(lane < n_sel, score, -jnp.inf)
        idx_row = jnp.zeros((1, LANE), jnp.int32)
        for it in range(NSA_TOPN):
            m = jnp.max(score, axis=-1, keepdims=True)
            j = jnp.min(jnp.where(score == m, lane, 1 << 30), axis=-1, keepdims=True)
            idx_row = jnp.where(lane_out == it, j, idx_row)
            score = jnp.where(lane == j, -jnp.inf, score)
        idx_ref[0, :, g * LANE:(g + 1) * LANE] = idx_row


def _nsa_sample_sel_kernel(idx_ref, pt_ref, q_ref, kn_ref, vn_ref, k0_ref, k1_ref, v0_ref, v1_ref, b0_ref, b1_ref,
                           o_ref, m_ref, l_ref, acc_ref, *, cur, n_past_blk):
    b = pl.program_id(0)
    n = pl.program_id(1)
    scale = HEAD_DIM ** -0.5

    @pl.when(n == 0)
    def _():
        m_ref[...] = jnp.full(m_ref.shape, NEG, F32)
        l_ref[...] = jnp.zeros_like(l_ref)
        acc_ref[...] = jnp.zeros_like(acc_ref)

    row0 = lax.broadcasted_iota(jnp.int32, (NSA_BLOCK, HEAD_DIM), 0) == 0
    for g, (k_ref, v_ref, bias_ref) in enumerate(((k0_ref, v0_ref, b0_ref), (k1_ref, v1_ref, b1_ref))):
        i = idx_ref[(b * NSA_KV + g) * NSA_TOPN + n]
        is_new = i >= n_past_blk
        knew = jnp.where(row0, kn_ref[0, :, g * HEAD_DIM:(g + 1) * HEAD_DIM], 0.0)
        vnew = jnp.where(row0, vn_ref[0, :, g * HEAD_DIM:(g + 1) * HEAD_DIM], 0.0)
        kblk = jnp.where(is_new, knew, k_ref[0]).astype(BF16)
        vblk = jnp.where(is_new, vnew, v_ref[0]).astype(BF16)
        qg = jnp.concatenate([q_ref[0, :, (g * NSA_REP + r) * HEAD_DIM:(g * NSA_REP + r + 1) * HEAD_DIM]
                              for r in range(NSA_REP)], axis=0) * scale
        rows = slice(g * NSA_REP, (g + 1) * NSA_REP)
        bias = bias_ref[0, rows, :]
        ok = (bias > 0.5 * NEG) & (i <= cur)
        ls = jnp.where(ok, _dot_nt(qg.astype(BF16), kblk) + bias, NEG)
        m_old = m_ref[rows, :]
        m_new = jnp.maximum(m_old, jnp.max(ls, axis=-1, keepdims=True))
        alpha = jnp.exp(m_old - m_new)
        ps = jnp.where(ok, jnp.exp(ls - m_new[:, 0:1]), 0.0)
        l_ref[rows, :] = alpha * l_ref[rows, :] + jnp.sum(ps, axis=-1, keepdims=True)
        acc_ref[rows, :] = alpha * acc_ref[rows, :] + _dot(ps.astype(BF16), vblk)
        m_ref[rows, :] = m_new

    @pl.when(n == pl.num_programs(1) - 1)
    def _():
        out = acc_ref[...] / jnp.maximum(l_ref[...], 1e-30)
        for h in range(ATT_HEADS):
            o_ref[0, :, h * HEAD_DIM:(h + 1) * HEAD_DIM] = out[h:h + 1, :]


def _nsa_combine_kernel(gate_ref, c_ref, s_ref, w_ref, o_ref):
    gates = _sigmoid(gate_ref[...])
    for h in range(ATT_HEADS):
        cols = slice(h * HEAD_DIM, (h + 1) * HEAD_DIM)
        out = (gates[:, 3 * h:3 * h + 1] * c_ref[:, cols] + gates[:, 3 * h + 1:3 * h + 2] * s_ref[:, cols]
               + gates[:, 3 * h + 2:3 * h + 3] * w_ref[:, cols])
        o_ref[:, cols] = out.astype(BF16)


def nsa_sample_attention(proj, kc_blk, vc_blk, sel_k_pool, sel_v_pool, win_k, win_v, page_table, table, past):
    nb = proj.shape[0]
    n_pages = page_table.shape[1]
    n_cmp = kc_blk.shape[0] // nb
    sub = PAGE_SIZE // NSA_BLOCK
    n_past_blk = n_pages * sub
    cur = past // NSA_BLOCK
    n_sel = -(-(past + 1) // NSA_BLOCK)
    proj3 = proj.reshape(nb, 1, -1)
    full = lambda shape: pl.BlockSpec(shape, lambda *_: (0,) * len(shape))
    dist_c = past - (jnp.arange(n_cmp) * NSA_BLOCK + NSA_BLOCK - 1)
    bias_c = jnp.where(dist_c >= 0, table.T[:, rel_bucket(dist_c)].astype(F32), NEG)
    o_cmp, idx = pl.pallas_call(
        functools.partial(_nsa_sample_cmp_kernel, cur=cur, n_sel=n_sel),
        grid=(nb,),
        in_specs=[
            pl.BlockSpec((1, 1, NSA_QD), lambda b: (b, 0, 0)),
            pl.BlockSpec((1, n_cmp, NSA_KVW), lambda b: (b, 0, 0)),
            pl.BlockSpec((1, n_cmp, NSA_KVW), lambda b: (b, 0, 0)),
            full(bias_c.shape),
        ],
        out_specs=[pl.BlockSpec((1, 1, NSA_QD), lambda b: (b, 0, 0)), pl.BlockSpec((1, 1, NSA_KV * LANE), lambda b: (b, 0, 0))],
        out_shape=[jax.ShapeDtypeStruct((nb, 1, NSA_QD), F32), jax.ShapeDtypeStruct((nb, 1, NSA_KV * LANE), jnp.int32)],
        compiler_params=_cparams("parallel"),
        name="nsa_sample_cmp",
    )(proj3, kc_blk.reshape(nb, n_cmp, NSA_KVW), vc_blk.reshape(nb, n_cmp, NSA_KVW), bias_c)
    idx_flat = idx.reshape(nb, NSA_KV, LANE)[:, :, :NSA_TOPN].reshape(-1)

    kpos = jnp.arange(n_sel * NSA_BLOCK).reshape(n_sel, 1, NSA_BLOCK)
    dist_s = past - kpos
    bias_s = jnp.where(dist_s >= 0, jnp.moveaxis(table[rel_bucket(dist_s[:, 0, :])], -1, 1).astype(F32), NEG)

    def phys(b, n, g, idx_ref, pt_ref):
        i = jnp.minimum(idx_ref[(b * NSA_KV + g) * NSA_TOPN + n], n_past_blk - 1)
        return pt_ref[b, i // sub] * sub + i % sub

    pool_spec = lambda g: pl.BlockSpec((1, NSA_BLOCK, HEAD_DIM), lambda b, n, idx_ref, pt_ref: (phys(b, n, g, idx_ref, pt_ref), 0, g))
    bias_spec = lambda g: pl.BlockSpec(
        (1, ATT_HEADS, NSA_BLOCK), lambda b, n, idx_ref, pt_ref: (idx_ref[(b * NSA_KV + g) * NSA_TOPN + n], 0, 0))
    row_spec = lambda width, blk: pl.BlockSpec((1, 1, width), lambda b, n, idx_ref, pt_ref: (b, 0, blk))
    pool_k = sel_k_pool.reshape(-1, NSA_BLOCK, NSA_KVW)
    pool_v = sel_v_pool.reshape(-1, NSA_BLOCK, NSA_KVW)
    o_sel = pl.pallas_call(
        functools.partial(_nsa_sample_sel_kernel, cur=cur, n_past_blk=n_past_blk),
        grid_spec=pltpu.PrefetchScalarGridSpec(
            num_scalar_prefetch=2,
            grid=(nb, NSA_TOPN),
            in_specs=[row_spec(NSA_QD, 0), row_spec(NSA_KVW, NSA_QD // NSA_KVW + 2), row_spec(NSA_KVW, NSA_QD // NSA_KVW + 3),
                      pool_spec(0), pool_spec(1), pool_spec(0), pool_spec(1), bias_spec(0), bias_spec(1)],
            out_specs=row_spec(NSA_QD, 0),
            scratch_shapes=[pltpu.VMEM((ATT_HEADS, LANE), F32), pltpu.VMEM((ATT_HEADS, LANE), F32),
                            pltpu.VMEM((ATT_HEADS, HEAD_DIM), F32)],
        ),
        out_shape=jax.ShapeDtypeStruct((nb, 1, NSA_QD), F32),
        compiler_params=_cparams("parallel", "arbitrary"),
        name="nsa_sample_sel",
    )(idx_flat, page_table, proj3, proj3, proj3, pool_k, pool_k, pool_v, pool_v, bias_s, bias_s)

    o_win = decode_attention(proj, 0, NSA_QD // NSA_KVW + 4, NSA_QD // NSA_KVW + 5, win_k, win_v, NSA_KV, table, None, F32)
    row2 = lambda width, blk: pl.BlockSpec((nb, width), lambda i: (0, blk))
    return pl.pallas_call(
        _nsa_combine_kernel,
        grid=(1,),
        in_specs=[row2(LANE, NSA_GATE_BLK), row2(NSA_QD, 0), row2(NSA_QD, 0), row2(NSA_QD, 0)],
        out_specs=row2(NSA_QD, 0),
        out_shape=jax.ShapeDtypeStruct((nb, NSA_QD), BF16),
        compiler_params=_cparams("arbitrary"),
        name="nsa_combine",
    )(proj, o_cmp.reshape(nb, NSA_QD), o_sel.reshape(nb, NSA_QD), o_win)


def nsa_weights(w_in):
    return _pad_cols(w_in, _round_up(w_in.shape[1], COL_TILE)).astype(BF16)


def nsa_layer(xp, xs, batch, seq, past, norm_g_mix, table, w_in, pos_k, pos_v, w1_k, w2_k, w1_v, w2_v, w_o,
              cmp_k_pool, cmp_v_pool, sel_k_pool, sel_v_pool, win_k, win_v, page_table):
    w_in_b = nsa_weights(w_in)
    w_o_b = w_o.astype(BF16)
    kv0 = NSA_QD // NSA_KVW
    nb = xs.shape[0]

    proj_p = norm_proj(xp, norm_g_mix, w_in_b)
    kc_blk = compress_rows(proj_p, kv0, pos_k, w1_k, w2_k)
    vc_blk = compress_rows(proj_p, kv0 + 1, pos_v, w1_v, w2_v)
    n_prev = -(-NSA_WINDOW // ATT_BLOCK)
    bias_w = local_bias(table, n_prev, NSA_WINDOW, NSA_KV)
    o_win = banded_attention(proj_p, batch, seq, 0, kv0 + 4, kv0 + 5, NSA_KV, n_prev, bias_w, None, F32)
    o_p = nsa_prompt_attention(proj_p, o_win, kc_blk, vc_blk, table, batch, seq)
    xp = proj_res(o_p, w_o_b, xp)
    kvs_p = proj_p.reshape(batch, seq, -1)[:, :, NSA_QD:NSA_QD + 6 * NSA_KVW].reshape(batch, seq, 6, NSA_KV, HEAD_DIM)
    buf = min(NSA_WINDOW, seq)
    outs_p = [kvs_p[:, :, j] for j in range(4)] + [kvs_p[:, seq - buf:, 4], kvs_p[:, seq - buf:, 5]]

    proj_s = norm_proj(xs, norm_g_mix, w_in_b)
    kc_s = compress_paged(cmp_k_pool, page_table, pos_k, w1_k, w2_k)
    vc_s = compress_paged(cmp_v_pool, page_table, pos_v, w1_v, w2_v)
    o_s = nsa_sample_attention(proj_s, kc_s, vc_s, sel_k_pool, sel_v_pool, win_k, win_v, page_table, table, past)
    xs = proj_res(o_s, w_o_b, xs)
    kvs_s = proj_s[:, NSA_QD:NSA_QD + 6 * NSA_KVW].reshape(nb, 1, 6, NSA_KV, HEAD_DIM)
    outs_s = [kvs_s[:, :, j] for j in range(4)]
    outs_s += [jnp.concatenate([win_k[:, 1:], kvs_s[:, :, 4]], axis=1), jnp.concatenate([win_v[:, 1:], kvs_s[:, :, 5]], axis=1)]
    return xp, xs, outs_p, outs_s


def kernel(x_prompt, x_sample, state_ssd_conv, state_ssd, cache_swa_k, cache_swa_v, state_gdn_conv, state_gdn,
           cache_nsa_cmp_k, cache_nsa_cmp_v, cache_nsa_sel_k, cache_nsa_sel_v, cache_nsa_win_k, cache_nsa_win_v,
           page_table, rel_table, norm_ffn1, norm_mix, norm_ffn2, norm_final,
           ffn1_gate, ffn1_up, ffn1_down, ffn2_gate, ffn2_up, ffn2_down,
           ssd_w_in, ssd_conv_w, ssd_conv_b, ssd_dt_bias, ssd_a_log, ssd_d, ssd_norm, ssd_w_out,
           swa_w_qkv, swa_sink, swa_w_o,
           gdn_w_in, gdn_conv_w, gdn_dt_bias, gdn_a_log, gdn_norm, gdn_w_out,
           nsa_w_in, nsa_pos_k, nsa_pos_v, nsa_cmp_w1_k, nsa_cmp_w2_k, nsa_cmp_w1_v, nsa_cmp_w2_v, nsa_w_o):
    batch, seq, d = x_prompt.shape
    nb = x_sample.shape[0]
    depth = norm_mix.shape[0]
    past = page_table.shape[1] * PAGE_SIZE
    xp = x_prompt.reshape(batch * seq, d)
    xs = x_sample.reshape(nb, d)
    outs_p = {}
    outs_s = {}

    def put(store, name, value):
        store.setdefault(name, []).append(value)

    for i in range(depth):
        kind, li = i % 4, i // 4
        w1 = ffn_weights(ffn1_gate[i], ffn1_up[i], ffn1_down[i])
        xp = ffn(xp, norm_ffn1[i], *w1)
        xs = ffn(xs, norm_ffn1[i], *w1)
        if kind == 0:
            xp, xs, cp, sp, cs, ss = ssd_layer(
                xp, xs, batch, seq, norm_mix[i], ssd_w_in[li], ssd_conv_w[li], ssd_conv_b[li], ssd_dt_bias[li],
                ssd_a_log[li], ssd_d[li], ssd_norm[li], ssd_w_out[li], state_ssd_conv[li], state_ssd[li])
            put(outs_p, "ssd_conv", cp), put(outs_p, "ssd_state", sp)
            put(outs_s, "ssd_conv", cs), put(outs_s, "ssd_state", ss)
        elif kind == 1:
            xp, xs, kp, vp, kq, vq = swa_layer(xp, xs, batch, seq, norm_mix[i], rel_table, swa_w_qkv[li], swa_sink[li],
                                               swa_w_o[li], cache_swa_k[li], cache_swa_v[li])
            put(outs_p, "swa_k", kp), put(outs_p, "swa_v", vp)
            put(outs_s, "swa_k", kq), put(outs_s, "swa_v", vq)
        elif kind == 2:
            xp, xs, cp, sp, cs, ss = gdn_layer(
                xp, xs, batch, seq, norm_mix[i], gdn_w_in[li], gdn_conv_w[li], gdn_dt_bias[li], gdn_a_log[li],
                gdn_norm[li], gdn_w_out[li], state_gdn_conv[li], state_gdn[li])
            put(outs_p, "gdn_conv", cp), put(outs_p, "gdn_state", sp)
            put(outs_s, "gdn_conv", cs), put(outs_s, "gdn_state", ss)
        else:
            xp, xs, op, os_ = nsa_layer(
                xp, xs, batch, seq, past, norm_mix[i], rel_table, nsa_w_in[li], nsa_pos_k[li], nsa_pos_v[li],
                nsa_cmp_w1_k[li], nsa_cmp_w2_k[li], nsa_cmp_w1_v[li], nsa_cmp_w2_v[li], nsa_w_o[li],
                cache_nsa_cmp_k[li], cache_nsa_cmp_v[li], cache_nsa_sel_k[li], cache_nsa_sel_v[li],
                cache_nsa_win_k[li], cache_nsa_win_v[li], page_table)
            for j in range(6):
                put(outs_p, f"nsa{j}", op[j]), put(outs_s, f"nsa{j}", os_[j])
        w2 = ffn_weights(ffn2_gate[i], ffn2_up[i], ffn2_down[i])
        xp = ffn(xp, norm_ffn2[i], *w2)
        xs = ffn(xs, norm_ffn2[i], *w2)

    y_prompt = final_norm(xp, norm_final).reshape(batch, seq, d)
    y_sample = final_norm(xs, norm_final).reshape(nb, 1, d)
    order = ["ssd_conv", "ssd_state", "swa_k", "swa_v", "gdn_conv", "gdn_state"] + [f"nsa{j}" for j in range(6)]
    states_p = tuple(jnp.stack(outs_p[name]) for name in order)
    states_s = tuple(jnp.stack(outs_s[name]) for name in order)
    return (y_prompt, y_sample) + states_p + states_s
```

```python
import functools
import math

import jax
import jax.numpy as jnp
from jax import lax
from jax.experimental import pallas as pl
from jax.experimental.pallas import tpu as pltpu

F32 = jnp.float32
BF16 = jnp.bfloat16
HIGHEST = lax.Precision.HIGHEST

EPS = 1e-6
D_MODEL = 2048
D_FF = 5504
LANE = 128
VMEM_LIMIT = 56 * 1024 * 1024
NEG = -1e30

FF_TILE = 512
D_FF_PAD = -(-D_FF // FF_TILE) * FF_TILE
ROW_TILE = 512
COL_TILE = 512

SSD_INNER = 4096
SSD_HEADS = 64
SSD_P = 64
SSD_N = 128
SSD_GROUPS = 8
SSD_GW = SSD_INNER // SSD_GROUPS
SSD_CONV_DIM = SSD_INNER + 2 * SSD_GROUPS * SSD_N
SSD_CHUNK = 128


def _cparams(*sem):
    return pltpu.CompilerParams(dimension_semantics=sem, vmem_limit_bytes=VMEM_LIMIT)


def _pad_cols(w, n):
    return jnp.pad(w, ((0, 0), (0, n - w.shape[1])))


def _round_up(n, m):
    return -(-n // m) * m


def _sigmoid(x):
    return 1.0 / (1.0 + jnp.exp(-x))


def _silu(x):
    return x * _sigmoid(x)


def _softplus(x):
    return jnp.maximum(x, 0.0) + jnp.log(1.0 + jnp.exp(-jnp.abs(x)))


def _rms(x, g):
    return x * lax.rsqrt(jnp.mean(x * x, axis=-1, keepdims=True) + EPS) * g


def _dot(a, b, **kw):
    return jnp.dot(a, b, preferred_element_type=F32, **kw)


def _dot_nt(a, b):
    return lax.dot_general(a, b, (((1,), (1,)), ((), ())), preferred_element_type=F32)


def _dot_tn(a, b):
    return lax.dot_general(a, b, (((0,), (0,)), ((), ())), preferred_element_type=F32)


def _eye(n):
    r = lax.broadcasted_iota(jnp.int32, (n, n), 0)
    c = lax.broadcasted_iota(jnp.int32, (n, n), 1)
    return r == c


def _row_to_col(v, eye):
    n = v.shape[1]
    return jnp.sum(jnp.where(eye, jnp.broadcast_to(v, (n, n)), 0.0), axis=-1, keepdims=True)


def _col_to_row(v, eye):
    n = v.shape[0]
    return jnp.sum(jnp.where(eye, jnp.broadcast_to(v, (n, n)), 0.0), axis=0, keepdims=True)


def _ffn_kernel(x_ref, g_ref, wg_ref, wu_ref, wd_ref, o_ref, xn_ref, acc_ref):
    f = pl.program_id(1)

    @pl.when(f == 0)
    def _():
        xn_ref[...] = _rms(x_ref[...], g_ref[...]).astype(BF16)
        acc_ref[...] = jnp.zeros_like(acc_ref)

    xn = xn_ref[...]
    gate = _dot(xn, wg_ref[...])
    up = _dot(xn, wu_ref[...])
    h = (_silu(gate) * up).astype(BF16)
    acc_ref[...] += _dot(h, wd_ref[...])

    @pl.when(f == pl.num_programs(1) - 1)
    def _():
        o_ref[...] = x_ref[...] + 0.5 * acc_ref[...]


def ffn(x, g, wg, wu, wd, layer):
    m, d = x.shape
    tm = min(ROW_TILE, m)
    return pl.pallas_call(
        _ffn_kernel,
        grid=(m // tm, D_FF_PAD // FF_TILE),
        in_specs=[
            pl.BlockSpec((tm, d), lambda i, f: (i, 0)),
            pl.BlockSpec((1, d), lambda i, f: (0, 0)),
            pl.BlockSpec((None, d, FF_TILE), lambda i, f: (layer, 0, f)),
            pl.BlockSpec((None, d, FF_TILE), lambda i, f: (layer, 0, f)),
            pl.BlockSpec((None, FF_TILE, d), lambda i, f: (layer, f, 0)),
        ],
        out_specs=pl.BlockSpec((tm, d), lambda i, f: (i, 0)),
        out_shape=jax.ShapeDtypeStruct((m, d), F32),
        scratch_shapes=[pltpu.VMEM((tm, d), BF16), pltpu.VMEM((tm, d), F32)],
        compiler_params=_cparams("parallel", "arbitrary"),
        name="ffn",
    )(x, g.reshape(1, d), wg, wu, wd)


def _norm_proj_kernel(x_ref, g_ref, w_ref, o_ref, xn_ref):
    @pl.when(pl.program_id(1) == 0)
    def _():
        xn_ref[...] = _rms(x_ref[...], g_ref[...]).astype(BF16)

    o_ref[...] = _dot(xn_ref[...], w_ref[...])


def norm_proj(x, g, w):
    m, d = x.shape
    n = w.shape[1]
    tm = min(ROW_TILE, m)
    return pl.pallas_call(
        _norm_proj_kernel,
        grid=(m // tm, n // COL_TILE),
        in_specs=[
            pl.BlockSpec((tm, d), lambda i, j: (i, 0)),
            pl.BlockSpec((1, d), lambda i, j: (0, 0)),
            pl.BlockSpec((d, COL_TILE), lambda i, j: (0, j)),
        ],
        out_specs=pl.BlockSpec((tm, COL_TILE), lambda i, j: (i, j)),
        out_shape=jax.ShapeDtypeStruct((m, n), F32),
        scratch_shapes=[pltpu.VMEM((tm, d), BF16)],
        compiler_params=_cparams("parallel", "arbitrary"),
        name="norm_proj",
    )(x, g.reshape(1, d), w)


def _proj_res_kernel(a_ref, w_ref, r_ref, o_ref):
    o_ref[...] = r_ref[...] + _dot(a_ref[...], w_ref[...])


def proj_res(a, w, res):
    m, k = a.shape
    n = w.shape[1]
    tm = min(ROW_TILE, m)
    return pl.pallas_call(
        _proj_res_kernel,
        grid=(m // tm, n // COL_TILE),
        in_specs=[
            pl.BlockSpec((tm, k), lambda i, j: (i, 0)),
            pl.BlockSpec((k, COL_TILE), lambda i, j: (0, j)),
            pl.BlockSpec((tm, COL_TILE), lambda i, j: (i, j)),
        ],
        out_specs=pl.BlockSpec((tm, COL_TILE), lambda i, j: (i, j)),
        out_shape=jax.ShapeDtypeStruct((m, n), F32),
        compiler_params=_cparams("parallel", "parallel"),
        name="proj_res",
    )(a, w, res)


def _final_norm_kernel(x_ref, g_ref, o_ref):
    o_ref[...] = _rms(x_ref[...], g_ref[...])


def final_norm(x, g):
    m, d = x.shape
    tm = min(ROW_TILE, m)
    return pl.pallas_call(
        _final_norm_kernel,
        grid=(m // tm,),
        in_specs=[pl.BlockSpec((tm, d), lambda i: (i, 0)), pl.BlockSpec((1, d), lambda i: (0, 0))],
        out_specs=pl.BlockSpec((tm, d), lambda i: (i, 0)),
        out_shape=jax.ShapeDtypeStruct((m, d), F32),
        compiler_params=_cparams("parallel"),
        name="final_norm",
    )(x, g.reshape(1, d))


def _head_expand(width):
    r = lax.broadcasted_iota(jnp.int32, (LANE, width), 0)
    c = lax.broadcasted_iota(jnp.int32, (LANE, width), 1)
    return jnp.where(lax.shift_right_logical(c, 6) == r, 1.0, 0.0).astype(F32)


def _ssd_prompt_kernel(z_ref, x_ref, bc_ref, dt_ref, cw_ref, cb_ref, dtb_ref, alog_ref, dx_ref, ng_ref,
                       y_ref, st_ref, xpad_ref, s_ref, ybuf_ref):
    c = pl.program_id(1)
    L = SSD_CHUNK

    @pl.when(c == 0)
    def _():
        xpad_ref[0:8, :] = jnp.zeros((8, SSD_CONV_DIM), F32)
        s_ref[...] = jnp.zeros_like(s_ref)

    xpad_ref[8:8 + L, 0:SSD_INNER] = x_ref[...]
    xpad_ref[8:8 + L, SSD_INNER:SSD_CONV_DIM] = bc_ref[...]
    conv = cb_ref[...] + cw_ref[3:4, :] * xpad_ref[8:8 + L, :]
    for j in range(3):
        conv = conv + cw_ref[j:j + 1, :] * xpad_ref[pl.ds(5 + j, L), :]
    xpad_ref[0:8, :] = xpad_ref[L:L + 8, :]
    act = _silu(conv)
    xs = act[:, 0:SSD_INNER]
    bm = act[:, SSD_INNER:SSD_INNER + SSD_GROUPS * SSD_N]
    cm = act[:, SSD_INNER + SSD_GROUPS * SSD_N:SSD_CONV_DIM]

    dt = _softplus(dt_ref[...] + dtb_ref[...])
    da = dt * (-jnp.exp(alog_ref[...]))
    row = lax.broadcasted_iota(jnp.int32, (L, L), 0)
    col = lax.broadcasted_iota(jnp.int32, (L, L), 1)
    causal = row >= col
    cs = _dot(jnp.where(causal, 1.0, 0.0).astype(F32), da, precision=HIGHEST)
    cs_t = cs.T
    expand = _head_expand(SSD_INNER)
    dtx = _dot(dt, expand, precision=HIGHEST)
    csx = _dot(cs, expand, precision=HIGHEST)
    xdt = xs * dtx
    cs_last = csx[L - 1:L, :]
    xdt_end = (xdt * jnp.exp(cs_last - csx)).astype(BF16)
    xdt_b = xdt.astype(BF16)
    ecs = jnp.exp(csx)
    dec_last = jnp.exp(cs_last)
    lane = lax.broadcasted_iota(jnp.int32, (L, LANE), 1)

    for g in range(SSD_GROUPS):
        lo_n = g * SSD_N
        lo_f = g * SSD_GW
        bg = bm[:, lo_n:lo_n + SSD_N]
        cg = cm[:, lo_n:lo_n + SSD_N].astype(BF16)
        cb = _dot_nt(cg, bg.astype(BF16))
        sg = s_ref[:, lo_f:lo_f + SSD_GW]
        yoff = _dot(cg, sg.astype(BF16)) * ecs[:, lo_f:lo_f + SSD_GW]
        for pr in range(SSD_GW // LANE):
            lo = lo_f + pr * LANE
            xp = xdt_b[:, lo:lo + LANE]
            outs = []
            for hh in (g * 8 + 2 * pr, g * 8 + 2 * pr + 1):
                diff = cs[:, hh:hh + 1] - cs_t[hh:hh + 1, :]
                dec = jnp.exp(jnp.where(causal, diff, NEG))
                outs.append(_dot((cb * dec).astype(BF16), xp))
            ybuf_ref[:, lo:lo + LANE] = jnp.where(lane < SSD_P, outs[0], outs[1]) + yoff[:, pr * LANE:(pr + 1) * LANE]
        s_ref[:, lo_f:lo_f + SSD_GW] = (sg * dec_last[:, lo_f:lo_f + SSD_GW]
                                        + _dot(bg.T.astype(BF16), xdt_end[:, lo_f:lo_f + SSD_GW]))

    y = ybuf_ref[...] + xs * dx_ref[...]
    yz = y * _silu(z_ref[...])
    for g in range(SSD_GROUPS):
        lo_f = g * SSD_GW
        y_ref[:, lo_f:lo_f + SSD_GW] = _rms(yz[:, lo_f:lo_f + SSD_GW], ng_ref[:, lo_f:lo_f + SSD_GW]).astype(BF16)

    @pl.when(c == pl.num_programs(1) - 1)
    def _():
        for k in range(SSD_INNER // LANE):
            st_ref[0, k * LANE:(k + 1) * LANE, :] = s_ref[:, k * LANE:(k + 1) * LANE].T


def _pad_lanes(v, n=LANE):
    v = v.reshape(1, -1).astype(F32)
    return jnp.pad(v, ((0, 0), (0, n - v.shape[1])))


def ssd_prompt(proj, batch, seq, conv_w, conv_b, dt_bias, a_log, d_skip, norm_g):
    L = SSD_CHUNK
    nc = seq // L
    rowb = lambda b, c: b * nc + c
    full = lambda shape: pl.BlockSpec(shape, lambda b, c: (0,) * len(shape))
    y, st = pl.pallas_call(
        _ssd_prompt_kernel,
        grid=(batch, nc),
        in_specs=[
            pl.BlockSpec((L, SSD_INNER), lambda b, c: (rowb(b, c), 0)),
            pl.BlockSpec((L, SSD_INNER), lambda b, c: (rowb(b, c), 1)),
            pl.BlockSpec((L, 2 * SSD_GROUPS * SSD_N), lambda b, c: (rowb(b, c), 4)),
            pl.BlockSpec((L, LANE), lambda b, c: (rowb(b, c), (SSD_INNER + SSD_CONV_DIM) // LANE)),
            full((4, SSD_CONV_DIM)), full((1, SSD_CONV_DIM)), full((1, LANE)), full((1, LANE)),
            full((1, SSD_INNER)), full((1, SSD_INNER)),
        ],
        out_specs=[
            pl.BlockSpec((L, SSD_INNER), lambda b, c: (rowb(b, c), 0)),
            pl.BlockSpec((1, SSD_INNER, SSD_N), lambda b, c: (b, 0, 0)),
        ],
        out_shape=[
            jax.ShapeDtypeStruct((batch * seq, SSD_INNER), BF16),
            jax.ShapeDtypeStruct((batch, SSD_INNER, SSD_N), F32),
        ],
        scratch_shapes=[
            pltpu.VMEM((L + 8, SSD_CONV_DIM), F32),
            pltpu.VMEM((SSD_N, SSD_INNER), F32),
            pltpu.VMEM((L, SSD_INNER), F32),
        ],
        compiler_params=_cparams("parallel", "arbitrary"),
        name="ssd_prompt",
    )(proj, proj, proj, proj, conv_w, conv_b.reshape(1, -1), _pad_lanes(dt_bias), _pad_lanes(a_log),
      jnp.repeat(d_skip, SSD_P).reshape(1, -1), norm_g.reshape(1, -1))
    return y, st.reshape(batch, SSD_HEADS, SSD_P, SSD_N)


def _ssd_sample_pre_kernel(x_ref, bc_ref, dt_ref, prev_ref, cw_ref, cb_ref, dtb_ref, alog_ref,
                           xs_ref, b_ref, c_ref, xdt_ref, dec_ref):
    xnew = jnp.concatenate([x_ref[...], bc_ref[...]], axis=1)
    conv = cb_ref[...] + cw_ref[3:4, :] * xnew
    for j in range(3):
        conv = conv + cw_ref[j:j + 1, :] * prev_ref[j]
    act = _silu(conv)
    xs = act[:, 0:SSD_INNER]
    dt = _softplus(dt_ref[...] + dtb_ref[...])
    da = dt * (-jnp.exp(alog_ref[...]))
    expand = _head_expand(SSD_INNER)
    xs_ref[...] = xs
    b_ref[...] = act[:, SSD_INNER:SSD_INNER + SSD_GROUPS * SSD_N]
    c_ref[...] = act[:, SSD_INNER + SSD_GROUPS * SSD_N:SSD_CONV_DIM]
    xdt_ref[...] = xs * _dot(dt, expand, precision=HIGHEST)
    dec_ref[...] = jnp.exp(_dot(da, expand, precision=HIGHEST))


def _ssd_sample_state_kernel(s_ref, xdt_ref, dec_ref, b_ref, c_ref, so_ref, y_ref):
    eye = _eye(LANE)
    for k in range(SSD_INNER // LANE):
        g = k // (SSD_GW // LANE)
        lo = k * LANE
        xcol = _row_to_col(xdt_ref[0, :, lo:lo + LANE], eye)
        dcol = _row_to_col(dec_ref[0, :, lo:lo + LANE], eye)
        brow = b_ref[0, :, g * SSD_N:(g + 1) * SSD_N]
        crow = c_ref[0, :, g * SSD_N:(g + 1) * SSD_N]
        snew = dcol * s_ref[0, lo:lo + LANE, :] + xcol * brow
        so_ref[0, lo:lo + LANE, :] = snew
        ycol = jnp.sum(snew * crow, axis=-1, keepdims=True)
        y_ref[0, :, lo:lo + LANE] = _col_to_row(ycol, eye)


def _ssd_sample_post_kernel(y_ref, xs_ref, z_ref, dx_ref, ng_ref, o_ref):
    y = y_ref[...] + xs_ref[...] * dx_ref[...]
    yz = y * _silu(z_ref[...])
    for g in range(SSD_GROUPS):
        lo_f = g * SSD_GW
        o_ref[:, lo_f:lo_f + SSD_GW] = _rms(yz[:, lo_f:lo_f + SSD_GW], ng_ref[:, lo_f:lo_f + SSD_GW]).astype(BF16)


def ssd_sample(proj, conv_prev, state, conv_w, conv_b, dt_bias, a_log, d_skip, norm_g):
    nb = proj.shape[0]
    full = lambda shape: pl.BlockSpec(shape, lambda *_: (0,) * len(shape))
    row_spec = lambda width, blk: pl.BlockSpec((nb, width), lambda *_: (0, blk))
    sds = lambda width: jax.ShapeDtypeStruct((nb, width), F32)
    gn = SSD_GROUPS * SSD_N
    xs, bm, cm, xdt, dec = pl.pallas_call(
        _ssd_sample_pre_kernel,
        grid=(1,),
        in_specs=[
            row_spec(SSD_INNER, 1), row_spec(2 * gn, 4), row_spec(LANE, (SSD_INNER + SSD_CONV_DIM) // LANE),
            full((3, nb, SSD_CONV_DIM)), full((4, SSD_CONV_DIM)), full((1, SSD_CONV_DIM)), full((1, LANE)), full((1, LANE)),
        ],
        out_specs=[full((nb, SSD_INNER)), full((nb, gn)), full((nb, gn)), full((nb, SSD_INNER)), full((nb, SSD_INNER))],
        out_shape=[sds(SSD_INNER), sds(gn), sds(gn), sds(SSD_INNER), sds(SSD_INNER)],
        compiler_params=_cparams("arbitrary"),
        name="ssd_sample_pre",
    )(proj, proj, proj, jnp.swapaxes(conv_prev, 0, 1), conv_w, conv_b.reshape(1, -1), _pad_lanes(dt_bias), _pad_lanes(a_log))
    per_row = lambda width: pl.BlockSpec((1, 1, width), lambda b: (b, 0, 0))
    s_new, y = pl.pallas_call(
        _ssd_sample_state_kernel,
        grid=(nb,),
        in_specs=[
            pl.BlockSpec((1, SSD_INNER, SSD_N), lambda b: (b, 0, 0)),
            per_row(SSD_INNER), per_row(SSD_INNER), per_row(gn), per_row(gn),
        ],
        out_specs=[pl.BlockSpec((1, SSD_INNER, SSD_N), lambda b: (b, 0, 0)), per_row(SSD_INNER)],
        out_shape=[jax.ShapeDtypeStruct((nb, SSD_INNER, SSD_N), F32), jax.ShapeDtypeStruct((nb, 1, SSD_INNER), F32)],
        compiler_params=_cparams("parallel"),
        name="ssd_sample_state",
    )(state.reshape(nb, SSD_INNER, SSD_N), xdt.reshape(nb, 1, -1), dec.reshape(nb, 1, -1),
      bm.reshape(nb, 1, -1), cm.reshape(nb, 1, -1))
    y = y.reshape(nb, SSD_INNER)
    out = pl.pallas_call(
        _ssd_sample_post_kernel,
        grid=(1,),
        in_specs=[full((nb, SSD_INNER)), full((nb, SSD_INNER)), row_spec(SSD_INNER, 0), full((1, SSD_INNER)), full((1, SSD_INNER))],
        out_specs=full((nb, SSD_INNER)),
        out_shape=jax.ShapeDtypeStruct((nb, SSD_INNER), BF16),
        compiler_params=_cparams("arbitrary"),
        name="ssd_sample_post",
    )(y, xs, proj, jnp.repeat(d_skip, SSD_P).reshape(1, -1), norm_g.reshape(1, -1))
    return out, s_new.reshape(nb, SSD_HEADS, SSD_P, SSD_N)


def ssd_weights(w_in):
    n = _round_up(w_in.shape[1], COL_TILE)
    return _pad_cols(w_in, n).astype(BF16)


def ssd_layer(xp, xs, batch, seq, norm_g_mix, w_in, conv_w, conv_b, dt_bias, a_log, d_skip, norm_g, w_out,
              conv_state, ssm_state):
    w_in_b = ssd_weights(w_in)
    w_out_b = w_out.astype(BF16)
    args = (conv_w, conv_b, dt_bias, a_log, d_skip, norm_g)
    proj_p = norm_proj(xp, norm_g_mix, w_in_b)
    y_p, st_p = ssd_prompt(proj_p, batch, seq, *args)
    xp = proj_res(y_p, w_out_b, xp)
    raw_p = proj_p.reshape(batch, seq, -1)[:, seq - 3:, SSD_INNER:SSD_INNER + SSD_CONV_DIM]
    proj_s = norm_proj(xs, norm_g_mix, w_in_b)
    y_s, st_s = ssd_sample(proj_s, conv_state, ssm_state, *args)
    xs = proj_res(y_s, w_out_b, xs)
    raw_s = jnp.concatenate([conv_state[:, 1:], proj_s[:, None, SSD_INNER:SSD_INNER + SSD_CONV_DIM]], axis=1)
    return xp, xs, raw_p, st_p, raw_s, st_s


CAST_ROWS = 128


def _cast_pad_kernel(x_ref, o_ref, *, row_tiles):
    o_ref[...] = jnp.zeros_like(o_ref)

    @pl.when(pl.program_id(1) < row_tiles)
    def _():
        o_ref[:, 0:x_ref.shape[1]] = x_ref[...].astype(BF16)


def cast_pad(w, rows, cols):
    depth, r, c = w.shape
    row_tiles = r // CAST_ROWS
    return pl.pallas_call(
        functools.partial(_cast_pad_kernel, row_tiles=row_tiles),
        grid=(depth, rows // CAST_ROWS),
        in_specs=[pl.BlockSpec((None, CAST_ROWS, c), lambda l, i: (l, jnp.minimum(i, row_tiles - 1), 0))],
        out_specs=pl.BlockSpec((None, CAST_ROWS, cols), lambda l, i: (l, i, 0)),
        out_shape=jax.ShapeDtypeStruct((depth, rows, cols), BF16),
        compiler_params=_cparams("parallel", "parallel"),
        name="cast_pad",
    )(w)


def ffn_weights(w_gate, w_up, w_down):
    d = w_gate.shape[1]
    return cast_pad(w_gate, d, D_FF_PAD), cast_pad(w_up, d, D_FF_PAD), cast_pad(w_down, D_FF_PAD, d)


HEAD_DIM = 128
ATT_HEADS = 16
ATT_BLOCK = 128
REL_BUCKETS = 32
REL_MAX_DIST = 128


def rel_bucket(dist):
    exact = REL_BUCKETS // 2
    d = jnp.maximum(dist, 0)
    logd = jnp.log(jnp.maximum(d, 1).astype(F32) / exact)
    far = exact + (logd / math.log(REL_MAX_DIST / exact) * (REL_BUCKETS - exact)).astype(jnp.int32)
    return jnp.where(d < exact, d, jnp.minimum(far, REL_BUCKETS - 1))


def head_bias(table, dist):
    onehot = (rel_bucket(dist)[..., None] == jnp.arange(REL_BUCKETS)).astype(F32)
    out = jnp.dot(onehot, table.astype(F32), precision=HIGHEST)
    return jnp.moveaxis(out, -1, 0)


def local_bias(table, n_prev, window, groups):
    width = (n_prev + 1) * ATT_BLOCK
    dist = n_prev * ATT_BLOCK + jnp.arange(ATT_BLOCK)[:, None] - jnp.arange(width)[None, :]
    bias = jnp.where((dist >= 0) & (dist <= window), head_bias(table, dist), NEG)
    return bias.reshape(groups, (ATT_HEADS // groups) * ATT_BLOCK, width)


def _banded_kernel(q_ref, k_ref, v_ref, bias_ref, sink_ref, o_ref, *, groups, n_prev, use_sink):
    qi = pl.program_id(1)
    rep = ATT_HEADS // groups
    scale = HEAD_DIM ** -0.5
    for g in range(groups):
        qs = jnp.concatenate(
            [q_ref[:, (g * rep + r) * HEAD_DIM:(g * rep + r + 1) * HEAD_DIM] for r in range(rep)], axis=0)
        qs = (qs * scale).astype(BF16)
        logits = []
        vals = []
        for s in range(n_prev + 1):
            kb = qi - n_prev + s
            start = pl.multiple_of(jnp.maximum(kb, 0) * ATT_BLOCK, ATT_BLOCK)
            kblk = k_ref[pl.ds(start, ATT_BLOCK), g * HEAD_DIM:(g + 1) * HEAD_DIM].astype(BF16)
            vals.append(v_ref[pl.ds(start, ATT_BLOCK), g * HEAD_DIM:(g + 1) * HEAD_DIM].astype(BF16))
            l_s = _dot_nt(qs, kblk) + bias_ref[g, :, s * ATT_BLOCK:(s + 1) * ATT_BLOCK]
            logits.append(jnp.where(kb >= 0, l_s, NEG))
        m = functools.reduce(jnp.maximum, [jnp.max(l, axis=-1, keepdims=True) for l in logits])
        if use_sink:
            sink = sink_ref[g][:, 0:1]
            m = jnp.maximum(m, sink)
            denom = jnp.exp(sink - m)
        else:
            denom = jnp.zeros_like(m)
        acc = jnp.zeros((rep * ATT_BLOCK, HEAD_DIM), F32)
        for l_s, vblk in zip(logits, vals):
            p = jnp.exp(l_s - m)
            denom = denom + jnp.sum(p, axis=-1, keepdims=True)
            acc = acc + _dot(p.astype(BF16), vblk)
        out = acc / denom
        for r in range(rep):
            h = g * rep + r
            o_ref[:, h * HEAD_DIM:(h + 1) * HEAD_DIM] = out[r * ATT_BLOCK:(r + 1) * ATT_BLOCK, :].astype(o_ref.dtype)


def banded_attention(proj, batch, seq, q_blk, k_blk, v_blk, groups, n_prev, bias, sink_rows, out_dtype):
    nq = seq // ATT_BLOCK
    kvw = groups * HEAD_DIM
    rep = ATT_HEADS // groups
    use_sink = sink_rows is not None
    if sink_rows is None:
        sink_rows = jnp.zeros((groups, rep * ATT_BLOCK, LANE), F32)
    return pl.pallas_call(
        functools.partial(_banded_kernel, groups=groups, n_prev=n_prev, use_sink=use_sink),
        grid=(batch, nq),
        in_specs=[
            pl.BlockSpec((ATT_BLOCK, ATT_HEADS * HEAD_DIM), lambda b, i: (b * nq + i, q_blk)),
            pl.BlockSpec((seq, kvw), lambda b, i: (b, k_blk)),
            pl.BlockSpec((seq, kvw), lambda b, i: (b, v_blk)),
            pl.BlockSpec(bias.shape, lambda b, i: (0, 0, 0)),
            pl.BlockSpec(sink_rows.shape, lambda b, i: (0, 0, 0)),
        ],
        out_specs=pl.BlockSpec((ATT_BLOCK, ATT_HEADS * HEAD_DIM), lambda b, i: (b * nq + i, 0)),
        out_shape=jax.ShapeDtypeStruct((batch * seq, ATT_HEADS * HEAD_DIM), out_dtype),
        compiler_params=_cparams("parallel", "arbitrary"),
        name="banded_attention",
    )(proj, proj, proj, bias, sink_rows)


def _decode_kernel(q_ref, kn_ref, vn_ref, kb_ref, vb_ref, bias_ref, bnew_ref, sink_ref, o_ref, *, groups, use_sink):
    rep = ATT_HEADS // groups
    scale = HEAD_DIM ** -0.5
    kall = kb_ref[0].astype(BF16)
    vall = vb_ref[0].astype(BF16)
    row_head = lax.rem(lax.broadcasted_iota(jnp.int32, (1, kall.shape[0]), 1), groups)
    for g in range(groups):
        qg = jnp.concatenate(
            [q_ref[0, :, (g * rep + r) * HEAD_DIM:(g * rep + r + 1) * HEAD_DIM] for r in range(rep)], axis=0)
        qg = qg * scale
        knew = kn_ref[0, :, g * HEAD_DIM:(g + 1) * HEAD_DIM]
        vnew = vn_ref[0, :, g * HEAD_DIM:(g + 1) * HEAD_DIM]
        l_c = jnp.where(row_head == g, _dot_nt(qg.astype(BF16), kall) + bias_ref[g * rep:(g + 1) * rep, :], NEG)
        l_n = jnp.sum(qg * knew, axis=-1, keepdims=True) + bnew_ref[g * rep:(g + 1) * rep, 0:1]
        m = jnp.maximum(jnp.max(l_c, axis=-1, keepdims=True), l_n)
        if use_sink:
            sink = sink_ref[g * rep:(g + 1) * rep, 0:1]
            m = jnp.maximum(m, sink)
            denom = jnp.exp(sink - m)
        else:
            denom = jnp.zeros_like(m)
        p_c = jnp.exp(l_c - m)
        p_n = jnp.exp(l_n - m)
        denom = denom + jnp.sum(p_c, axis=-1, keepdims=True) + p_n
        out = (_dot(p_c.astype(BF16), vall) + p_n * vnew) / denom
        for r in range(rep):
            h = g * rep + r
            o_ref[0, :, h * HEAD_DIM:(h + 1) * HEAD_DIM] = out[r:r + 1, :].astype(o_ref.dtype)


def decode_attention(proj, q_blk, k_blk, v_blk, k_buf, v_buf, groups, table, sink, out_dtype):
    nb, wn = k_buf.shape[0], k_buf.shape[1]
    kvw = groups * HEAD_DIM
    bias = jnp.repeat(head_bias(table, wn - jnp.arange(wn)), groups, axis=1)
    bnew = jnp.broadcast_to(table[0][:, None].astype(F32), (ATT_HEADS, LANE))
    use_sink = sink is not None
    sink_rows = jnp.broadcast_to((sink if use_sink else jnp.zeros((ATT_HEADS,), F32))[:, None].astype(F32), (ATT_HEADS, LANE))
    full = lambda shape: pl.BlockSpec(shape, lambda b: (0,) * len(shape))
    proj3 = proj.reshape(nb, 1, -1)
    out = pl.pallas_call(
        functools.partial(_decode_kernel, groups=groups, use_sink=use_sink),
        grid=(nb,),
        in_specs=[
            pl.BlockSpec((1, 1, ATT_HEADS * HEAD_DIM), lambda b: (b, 0, q_blk)),
            pl.BlockSpec((1, 1, kvw), lambda b: (b, 0, k_blk)),
            pl.BlockSpec((1, 1, kvw), lambda b: (b, 0, v_blk)),
            pl.BlockSpec((1, wn * groups, HEAD_DIM), lambda b: (b, 0, 0)),
            pl.BlockSpec((1, wn * groups, HEAD_DIM), lambda b: (b, 0, 0)),
            full((ATT_HEADS, wn * groups)), full((ATT_HEADS, LANE)), full((ATT_HEADS, LANE)),
        ],
        out_specs=pl.BlockSpec((1, 1, ATT_HEADS * HEAD_DIM), lambda b: (b, 0, 0)),
        out_shape=jax.ShapeDtypeStruct((nb, 1, ATT_HEADS * HEAD_DIM), out_dtype),
        compiler_params=_cparams("parallel"),
        name="decode_attention",
    )(proj3, proj3, proj3, k_buf.reshape(nb, wn * groups, HEAD_DIM), v_buf.reshape(nb, wn * groups, HEAD_DIM),
      bias, bnew, sink_rows)
    return out.reshape(nb, ATT_HEADS * HEAD_DIM)


SWA_KV = 4
SWA_WINDOW = 128


def swa_layer(xp, xs, batch, seq, norm_g_mix, table, w_qkv, sink, w_o, k_buf, v_buf):
    w_qkv_b = w_qkv.astype(BF16)
    w_o_b = w_o.astype(BF16)
    kvw = SWA_KV * HEAD_DIM
    qd = ATT_HEADS * HEAD_DIM
    rep = ATT_HEADS // SWA_KV
    n_prev = -(-SWA_WINDOW // ATT_BLOCK)
    bias = local_bias(table, n_prev, SWA_WINDOW, SWA_KV)
    sink_rows = jnp.broadcast_to(jnp.repeat(sink.astype(F32), ATT_BLOCK).reshape(SWA_KV, rep * ATT_BLOCK, 1),
                                 (SWA_KV, rep * ATT_BLOCK, LANE))
    proj_p = norm_proj(xp, norm_g_mix, w_qkv_b)
    o_p = banded_attention(proj_p, batch, seq, 0, qd // kvw, qd // kvw + 1, SWA_KV, n_prev, bias, sink_rows, BF16)
    xp = proj_res(o_p, w_o_b, xp)
    buf = min(SWA_WINDOW, seq)
    kv_p = proj_p.reshape(batch, seq, -1)[:, seq - buf:, qd:]
    k_p = kv_p[..., :kvw].reshape(batch, buf, SWA_KV, HEAD_DIM)
    v_p = kv_p[..., kvw:2 * kvw].reshape(batch, buf, SWA_KV, HEAD_DIM)
    proj_s = norm_proj(xs, norm_g_mix, w_qkv_b)
    o_s = decode_attention(proj_s, 0, qd // kvw, qd // kvw + 1, k_buf, v_buf, SWA_KV, table, sink, BF16)
    xs = proj_res(o_s, w_o_b, xs)
    nb = xs.shape[0]
    k_s = jnp.concatenate([k_buf[:, 1:], proj_s[:, None, qd:qd + kvw].reshape(nb, 1, SWA_KV, HEAD_DIM)], axis=1)
    v_s = jnp.concatenate([v_buf[:, 1:], proj_s[:, None, qd + kvw:qd + 2 * kvw].reshape(nb, 1, SWA_KV, HEAD_DIM)], axis=1)
    return xp, xs, k_p, v_p, k_s, v_s


GDN_K_HEADS = 16
GDN_V_HEADS = 32
GDN_D = 128
GDN_KEY = GDN_K_HEADS * GDN_D
GDN_VAL = GDN_V_HEADS * GDN_D
GDN_CONV_DIM = 2 * GDN_KEY + GDN_VAL
GDN_CHUNK = 64
GDN_BETA_BLK = (GDN_CONV_DIM + GDN_VAL) // LANE
GDN_GROUP = 8


def gdn_weights(w_in):
    c2 = GDN_CONV_DIM + GDN_VAL
    parts = [w_in[:, :c2], _pad_cols(w_in[:, c2:c2 + GDN_V_HEADS], LANE), _pad_cols(w_in[:, c2 + GDN_V_HEADS:], LANE)]
    w = jnp.concatenate(parts, axis=1)
    return _pad_cols(w, _round_up(w.shape[1], COL_TILE)).astype(BF16)


def _split_bf16(a):
    hi = a.astype(BF16)
    return hi, (a - hi.astype(F32)).astype(BF16)


def _dot3(a, b):
    ah, al = _split_bf16(a)
    bh, bl = _split_bf16(b)
    return _dot(ah, bh) + (_dot(ah, bl) + _dot(al, bh))


def _unit_lower_inverses(ms):
    n = ms[0].shape[0]
    eye = jnp.where(_eye(n), 1.0, 0.0)
    ps = [-m for m in ms]
    ts = [eye + p for p in ps]
    ps = [_dot3(p, p) for p in ps]
    k = 2
    while k < n:
        if 2 * k >= n:
            ts = [t + _dot3(t, p) for t, p in zip(ts, ps)]
        else:
            both = [_dot3(jnp.concatenate([t, p], axis=0), p) for t, p in zip(ts, ps)]
            ts = [t + b[:n] for t, b in zip(ts, both)]
            ps = [b[n:] for b in both]
        k *= 2
    return ts


def _l2n(x):
    return x * lax.rsqrt(jnp.sum(x * x, axis=-1, keepdims=True) + EPS)


def _gdn_prompt_kernel(qk_ref, v_ref, z_ref, beta_ref, a_ref, cw_ref, dtb_ref, alog_ref, ng_ref,
                       o_ref, st_ref, xpad_ref, s_ref):
    c = pl.program_id(1)
    L = GDN_CHUNK

    @pl.when(c == 0)
    def _():
        xpad_ref[0:8, :] = jnp.zeros((8, GDN_CONV_DIM), F32)
        s_ref[...] = jnp.zeros_like(s_ref)

    xpad_ref[8:8 + L, 0:2 * GDN_KEY] = qk_ref[...]
    xpad_ref[8:8 + L, 2 * GDN_KEY:GDN_CONV_DIM] = v_ref[...]
    conv = cw_ref[3:4, :] * xpad_ref[8:8 + L, :]
    for j in range(3):
        conv = conv + cw_ref[j:j + 1, :] * xpad_ref[pl.ds(5 + j, L), :]
    xpad_ref[0:8, :] = xpad_ref[L:L + 8, :]
    act = _silu(conv)

    beta = _sigmoid(beta_ref[...])
    g = -jnp.exp(alog_ref[...]) * _softplus(a_ref[...] + dtb_ref[...])
    row = lax.broadcasted_iota(jnp.int32, (L, L), 0)
    col = lax.broadcasted_iota(jnp.int32, (L, L), 1)
    incl = row >= col
    strict = row > col
    gc = _dot(jnp.where(incl, 1.0, 0.0).astype(F32), g, precision=HIGHEST)
    gc_t = gc.T
    eg = jnp.exp(gc)
    g_last = gc[L - 1:L, :]
    k_scale = jnp.exp(g_last - gc)
    s_scale = jnp.exp(g_last)
    ng = ng_ref[...]

    rep = GDN_V_HEADS // GDN_K_HEADS
    for grp in range(GDN_V_HEADS // GDN_GROUP):
        vhs = list(range(grp * GDN_GROUP, (grp + 1) * GDN_GROUP))
        khs = list(range(vhs[0] // rep, vhs[-1] // rep + 1))
        qn = {kh: _l2n(act[:, kh * GDN_D:(kh + 1) * GDN_D]) * GDN_D ** -0.5 for kh in khs}
        kn = {kh: _l2n(act[:, GDN_KEY + kh * GDN_D:GDN_KEY + (kh + 1) * GDN_D]) for kh in khs}
        kn_b = {kh: kn[kh].astype(BF16) for kh in khs}
        prods = {kh: _dot_nt(jnp.concatenate([kn_b[kh], qn[kh].astype(BF16)], axis=0), kn_b[kh]) for kh in khs}
        bcol = {vh: beta[:, vh:vh + 1] for vh in vhs}
        egcol = {vh: eg[:, vh:vh + 1] for vh in vhs}
        decay = {vh: jnp.exp(jnp.where(incl, gc[:, vh:vh + 1] - gc_t[vh:vh + 1, :], NEG)) for vh in vhs}
        t_inv = _unit_lower_inverses([jnp.where(strict, prods[vh // rep][:L] * bcol[vh] * decay[vh], 0.0) for vh in vhs])
        rhs = [jnp.concatenate([act[:, 2 * GDN_KEY + vh * GDN_D:2 * GDN_KEY + (vh + 1) * GDN_D] * bcol[vh],
                                kn[vh // rep] * (bcol[vh] * egcol[vh])], axis=1).astype(BF16) for vh in vhs]
        uw = [_dot(t.astype(BF16), r) for t, r in zip(t_inv, rhs)]
        s_old = [s_ref[vh] for vh in vhs]
        ws_qs = [_dot(jnp.concatenate([uw_h[:, GDN_D:], qn[vh // rep] * egcol[vh]], axis=0).astype(BF16), s.astype(BF16))
                 for vh, uw_h, s in zip(vhs, uw, s_old)]
        v_new = [(uw_h[:, :GDN_D] - wq[:L]).astype(BF16) for uw_h, wq in zip(uw, ws_qs)]
        outs = [wq[L:] + _dot((prods[vh // rep][L:] * decay[vh]).astype(BF16), vn)
                for vh, wq, vn in zip(vhs, ws_qs, v_new)]
        upd = [_dot_tn((kn[vh // rep] * k_scale[:, vh:vh + 1]).astype(BF16), vn) for vh, vn in zip(vhs, v_new)]
        for vh, s, u, o in zip(vhs, s_old, upd, outs):
            s_ref[vh] = s * s_scale[:, vh:vh + 1] + u
            zz = z_ref[:, vh * GDN_D:(vh + 1) * GDN_D]
            o_ref[:, vh * GDN_D:(vh + 1) * GDN_D] = (_rms(o, ng) * _silu(zz)).astype(BF16)

    @pl.when(c == pl.num_programs(1) - 1)
    def _():
        st_ref[0] = s_ref[...]


def gdn_prompt(proj, batch, seq, conv_w, dt_bias, a_log, norm_g):
    L = GDN_CHUNK
    nc = seq // L
    rowb = lambda b, c: b * nc + c
    full = lambda shape: pl.BlockSpec(shape, lambda b, c: (0,) * len(shape))
    return pl.pallas_call(
        _gdn_prompt_kernel,
        grid=(batch, nc),
        in_specs=[
            pl.BlockSpec((L, 2 * GDN_KEY), lambda b, c: (rowb(b, c), 0)),
            pl.BlockSpec((L, GDN_VAL), lambda b, c: (rowb(b, c), 1)),
            pl.BlockSpec((L, GDN_VAL), lambda b, c: (rowb(b, c), 2)),
            pl.BlockSpec((L, LANE), lambda b, c: (rowb(b, c), GDN_BETA_BLK)),
            pl.BlockSpec((L, LANE), lambda b, c: (rowb(b, c), GDN_BETA_BLK + 1)),
            full((4, GDN_CONV_DIM)), full((1, LANE)), full((1, LANE)), full((1, GDN_D)),
        ],
        out_specs=[
            pl.BlockSpec((L, GDN_VAL), lambda b, c: (rowb(b, c), 0)),
            pl.BlockSpec((1, GDN_V_HEADS, GDN_D, GDN_D), lambda b, c: (b, 0, 0, 0)),
        ],
        out_shape=[
            jax.ShapeDtypeStruct((batch * seq, GDN_VAL), BF16),
            jax.ShapeDtypeStruct((batch, GDN_V_HEADS, GDN_D, GDN_D), F32),
        ],
        scratch_shapes=[pltpu.VMEM((L + 8, GDN_CONV_DIM), F32), pltpu.VMEM((GDN_V_HEADS, GDN_D, GDN_D), F32)],
        compiler_params=_cparams("parallel", "arbitrary"),
        name="gdn_prompt",
    )(proj, proj, proj, proj, proj, conv_w, _pad_lanes(dt_bias), _pad_lanes(a_log), norm_g.reshape(1, -1))


def _gdn_sample_pre_kernel(qk_ref, v_ref, beta_ref, a_ref, prev_ref, cw_ref, dtb_ref, alog_ref,
                           q_ref, k_ref, vo_ref, bo_ref, eg_ref):
    xnew = jnp.concatenate([qk_ref[...], v_ref[...]], axis=1)
    conv = cw_ref[3:4, :] * xnew
    for j in range(3):
        conv = conv + cw_ref[j:j + 1, :] * prev_ref[j]
    act = _silu(conv)
    for kh in range(GDN_K_HEADS):
        q_ref[:, kh * GDN_D:(kh + 1) * GDN_D] = _l2n(act[:, kh * GDN_D:(kh + 1) * GDN_D]) * GDN_D ** -0.5
        k_ref[:, kh * GDN_D:(kh + 1) * GDN_D] = _l2n(act[:, GDN_KEY + kh * GDN_D:GDN_KEY + (kh + 1) * GDN_D])
    vo_ref[...] = act[:, 2 * GDN_KEY:]
    bo_ref[...] = _sigmoid(beta_ref[...])
    eg_ref[...] = jnp.exp(-jnp.exp(alog_ref[...]) * _softplus(a_ref[...] + dtb_ref[...]))


def _gdn_sample_state_kernel(s_ref, q_ref, k_ref, v_ref, z_ref, beta_ref, eg_ref, ng_ref, so_ref, o_ref):
    eye = _eye(GDN_D)
    ng = ng_ref[...]
    for kh in range(GDN_K_HEADS):
        qrow = q_ref[0, :, kh * GDN_D:(kh + 1) * GDN_D]
        krow = k_ref[0, :, kh * GDN_D:(kh + 1) * GDN_D]
        qcol = _row_to_col(qrow, eye)
        kcol = _row_to_col(krow, eye)
        qk = jnp.sum(qrow * krow, axis=-1, keepdims=True)
        for vh in (2 * kh, 2 * kh + 1):
            s_old = s_ref[0, vh]
            beta = beta_ref[0, :, vh:vh + 1]
            eg = eg_ref[0, :, vh:vh + 1]
            v = v_ref[0, :, vh * GDN_D:(vh + 1) * GDN_D]
            ks = jnp.sum(kcol * s_old, axis=0, keepdims=True)
            qs = jnp.sum(qcol * s_old, axis=0, keepdims=True)
            v_new = beta * v - (beta * eg) * ks
            o = eg * qs + qk * v_new
            so_ref[0, vh] = s_old * eg + kcol * v_new
            zz = z_ref[0, :, vh * GDN_D:(vh + 1) * GDN_D]
            o_ref[0, :, vh * GDN_D:(vh + 1) * GDN_D] = (_rms(o, ng) * _silu(zz)).astype(BF16)


def gdn_sample(proj, conv_prev, state, conv_w, dt_bias, a_log, norm_g):
    nb = proj.shape[0]
    full = lambda shape: pl.BlockSpec(shape, lambda *_: (0,) * len(shape))
    row_spec = lambda width, blk: pl.BlockSpec((nb, width), lambda *_: (0, blk))
    sds = lambda width: jax.ShapeDtypeStruct((nb, width), F32)
    qn, kn, v, beta, eg = pl.pallas_call(
        _gdn_sample_pre_kernel,
        grid=(1,),
        in_specs=[
            row_spec(2 * GDN_KEY, 0), row_spec(GDN_VAL, 1), row_spec(LANE, GDN_BETA_BLK), row_spec(LANE, GDN_BETA_BLK + 1),
            full((3, nb, GDN_CONV_DIM)), full((4, GDN_CONV_DIM)), full((1, LANE)), full((1, LANE)),
        ],
        out_specs=[full((nb, GDN_KEY)), full((nb, GDN_KEY)), full((nb, GDN_VAL)), full((nb, LANE)), full((nb, LANE))],
        out_shape=[sds(GDN_KEY), sds(GDN_KEY), sds(GDN_VAL), sds(LANE), sds(LANE)],
        compiler_params=_cparams("arbitrary"),
        name="gdn_sample_pre",
    )(proj, proj, proj, proj, jnp.swapaxes(conv_prev, 0, 1), conv_w, _pad_lanes(dt_bias), _pad_lanes(a_log))
    per_row = lambda width, blk=0: pl.BlockSpec((1, 1, width), lambda b: (b, 0, blk))
    st_spec = pl.BlockSpec((1, GDN_V_HEADS, GDN_D, GDN_D), lambda b: (b, 0, 0, 0))
    r3 = lambda a: a.reshape(nb, 1, -1)
    s_new, o = pl.pallas_call(
        _gdn_sample_state_kernel,
        grid=(nb,),
        in_specs=[st_spec, per_row(GDN_KEY), per_row(GDN_KEY), per_row(GDN_VAL), per_row(GDN_VAL, 2),
                  per_row(LANE), per_row(LANE), pl.BlockSpec((1, GDN_D), lambda b: (0, 0))],
        out_specs=[st_spec, per_row(GDN_VAL)],
        out_shape=[jax.ShapeDtypeStruct(state.shape, F32), jax.ShapeDtypeStruct((nb, 1, GDN_VAL), BF16)],
        compiler_params=_cparams("parallel"),
        name="gdn_sample_state",
    )(state, r3(qn), r3(kn), r3(v), r3(proj), r3(beta), r3(eg), norm_g.reshape(1, -1))
    return o.reshape(nb, GDN_VAL), s_new


def gdn_layer(xp, xs, batch, seq, norm_g_mix, w_in, conv_w, dt_bias, a_log, norm_g, w_out, conv_state, state):
    w_in_b = gdn_weights(w_in)
    w_out_b = w_out.astype(BF16)
    args = (conv_w, dt_bias, a_log, norm_g)
    proj_p = norm_proj(xp, norm_g_mix, w_in_b)
    o_p, st_p = gdn_prompt(proj_p, batch, seq, *args)
    xp = proj_res(o_p, w_out_b, xp)
    raw_p = proj_p.reshape(batch, seq, -1)[:, seq - 3:, :GDN_CONV_DIM]
    proj_s = norm_proj(xs, norm_g_mix, w_in_b)
    o_s, st_s = gdn_sample(proj_s, conv_state, state, *args)
    xs = proj_res(o_s, w_out_b, xs)
    raw_s = jnp.concatenate([conv_state[:, 1:], proj_s[:, None, :GDN_CONV_DIM]], axis=1)
    return xp, xs, raw_p, st_p, raw_s, st_s


NSA_KV = 2
NSA_BLOCK = 64
NSA_TOPN = 16
NSA_WINDOW = 512
NSA_HID = 256
NSA_FORCE = 1e4
NSA_REP = ATT_HEADS // NSA_KV
NSA_KVW = NSA_KV * HEAD_DIM
NSA_QD = ATT_HEADS * HEAD_DIM
NSA_GATE_BLK = (NSA_QD + 6 * NSA_KVW) // LANE
PAGE_SIZE = 128
PAGES_PER_STEP = 128
NSA_KEY_SPAN = 512


def _compress_block_rows(x_refs, pos_ref, w1_ref, w2_ref, o_ref):
    for g, x_ref in enumerate(x_refs):
        nblk = x_ref.shape[0] // NSA_BLOCK
        acc = jnp.zeros((nblk, NSA_HID), F32)
        for t in range(NSA_BLOCK):
            lhs = x_ref[pl.ds(t, nblk, stride=NSA_BLOCK), :] + pos_ref[t:t + 1, :]
            acc = acc + _dot(lhs.astype(BF16), w1_ref[t * HEAD_DIM:(t + 1) * HEAD_DIM, :])
        o_ref[:, g * HEAD_DIM:(g + 1) * HEAD_DIM] = _dot(_silu(acc).astype(BF16), w2_ref[...])


def _compress_rows_kernel(x0_ref, x1_ref, pos_ref, w1_ref, w2_ref, o_ref):
    _compress_block_rows((x0_ref, x1_ref), pos_ref, w1_ref, w2_ref, o_ref)


def compress_rows(proj, col_blk, pos, w1, w2):
    m = proj.shape[0]
    rows = min(m, 8192)
    full = lambda shape: pl.BlockSpec(shape, lambda i: (0,) * len(shape))
    head = lambda g: pl.BlockSpec((rows, HEAD_DIM), lambda i: (i, col_blk * NSA_KV + g))
    return pl.pallas_call(
        _compress_rows_kernel,
        grid=(m // rows,),
        in_specs=[head(0), head(1), full(pos.shape), full(w1.shape), full(w2.shape)],
        out_specs=pl.BlockSpec((rows // NSA_BLOCK, NSA_KVW), lambda i: (i, 0)),
        out_shape=jax.ShapeDtypeStruct((m // NSA_BLOCK, NSA_KVW), F32),
        compiler_params=_cparams("parallel"),
        name="compress_rows",
    )(proj, proj, pos, w1.astype(BF16), w2.astype(BF16))


SUB_BLOCKS = PAGE_SIZE // NSA_BLOCK
BLOCK_ROWS = NSA_BLOCK * NSA_KV


def _compress_paged_kernel(pt_ref, pool_ref, pos_ref, w1_ref, w2_ref, o_ref, buf_ref, sem, *, pages):
    i = pl.program_id(0)
    steps = pl.num_programs(0)
    nblk = pages * SUB_BLOCKS

    def block_copy(step, slot, k):
        page = pt_ref[step * pages + k // SUB_BLOCKS]
        start = pl.multiple_of((page * SUB_BLOCKS + lax.rem(k, SUB_BLOCKS)) * BLOCK_ROWS, BLOCK_ROWS)
        return pltpu.make_async_copy(pool_ref.at[pl.ds(start, BLOCK_ROWS), :], buf_ref.at[slot, :, k, :], sem.at[slot])

    def start_step(step, slot):
        def body(k, carry):
            block_copy(step, slot, k).start()
            return carry
        lax.fori_loop(0, nblk, body, 0)

    def wait_step(step, slot):
        def body(k, carry):
            block_copy(step, slot, k).wait()
            return carry
        lax.fori_loop(0, nblk, body, 0)

    slot = lax.rem(i, 2)

    @pl.when(i == 0)
    def _():
        start_step(0, 0)

    @pl.when(i + 1 < steps)
    def _():
        start_step(i + 1, 1 - slot)

    wait_step(i, slot)
    for g in range(NSA_KV):
        acc = jnp.zeros((nblk, NSA_HID), F32)
        for t in range(NSA_BLOCK):
            lhs = buf_ref[slot, NSA_KV * t + g] + pos_ref[t:t + 1, :]
            acc = acc + _dot(lhs.astype(BF16), w1_ref[t * HEAD_DIM:(t + 1) * HEAD_DIM, :])
        o_ref[:, g * HEAD_DIM:(g + 1) * HEAD_DIM] = _dot(_silu(acc).astype(BF16), w2_ref[...])


def compress_paged(pool, page_table, pos, w1, w2):
    pt = page_table.reshape(-1)
    pages = min(PAGES_PER_STEP, pt.shape[0])
    steps = pt.shape[0] // pages
    nblk = pages * SUB_BLOCKS
    full = lambda shape: pl.BlockSpec(shape, lambda i, pt: (0,) * len(shape))
    return pl.pallas_call(
        functools.partial(_compress_paged_kernel, pages=pages),
        grid_spec=pltpu.PrefetchScalarGridSpec(
            num_scalar_prefetch=1,
            grid=(steps,),
            in_specs=[pl.BlockSpec(memory_space=pl.ANY), full(pos.shape), full(w1.shape), full(w2.shape)],
            out_specs=pl.BlockSpec((nblk, NSA_KVW), lambda i, pt: (i, 0)),
            scratch_shapes=[pltpu.VMEM((2, BLOCK_ROWS, nblk, HEAD_DIM), F32), pltpu.SemaphoreType.DMA((2,))],
        ),
        out_shape=jax.ShapeDtypeStruct((steps * nblk, NSA_KVW), F32),
        compiler_params=_cparams("arbitrary"),
        name="compress_paged",
    )(pt, pool.reshape(-1, HEAD_DIM), pos, w1.astype(BF16), w2.astype(BF16))


def _top_n(score, lane, n):
    sel = jnp.zeros(score.shape, F32)
    for _ in range(n):
        m = jnp.max(score, axis=-1, keepdims=True)
        j = jnp.min(jnp.where(score == m, lane, 1 << 30), axis=-1, keepdims=True)
        pick = lane == j
        sel = jnp.where(pick, 1.0, sel)
        score = jnp.where(pick, -jnp.inf, score)
    return sel


def _nsa_prompt_kernel(q_ref, win_ref, gate_ref, kc_ref, vc_ref, ks_ref, vs_ref, bc_ref, bl_ref, far_ref, o_ref):
    qi = pl.program_id(1)
    seq = ks_ref.shape[0]
    n_cmp = kc_ref.shape[0]
    scale = HEAD_DIM ** -0.5
    gates = _sigmoid(gate_ref[...])
    qpos = qi * ATT_BLOCK + lax.broadcasted_iota(jnp.int32, (ATT_BLOCK, n_cmp), 0)
    blk = lax.broadcasted_iota(jnp.int32, (ATT_BLOCK, n_cmp), 1)
    cur = lax.shift_right_logical(qpos, 6)
    valid = blk <= cur
    forced = valid & ((blk == 0) | (blk >= cur - 1))
    qh_all, o_cmp_all, sel_all = [], [], []
    for g in range(NSA_KV):
        kcg = kc_ref[:, g * HEAD_DIM:(g + 1) * HEAD_DIM].astype(BF16)
        vcg = vc_ref[:, g * HEAD_DIM:(g + 1) * HEAD_DIM].astype(BF16)
        qh = [(q_ref[:, (g * NSA_REP + r) * HEAD_DIM:(g * NSA_REP + r + 1) * HEAD_DIM] * scale).astype(BF16)
              for r in range(NSA_REP)]
        bias_c = bc_ref[g, 0]
        ok_c = bias_c > 0.5 * NEG
        lc = _dot_nt(jnp.concatenate(qh, axis=0), kcg) + bias_c
        p = jnp.where(ok_c, jnp.exp(lc - jnp.max(lc, axis=-1, keepdims=True)), 0.0)
        p = p / jnp.maximum(jnp.sum(p, axis=-1, keepdims=True), 1e-30)
        p_slc = functools.reduce(lambda a, b: a + b, [p[r * ATT_BLOCK:(r + 1) * ATT_BLOCK] for r in range(NSA_REP)])
        score = jnp.where(forced, NSA_FORCE, jnp.where(valid, p_slc, -1.0))
        qh_all.append(qh)
        o_cmp_all.append(_dot(p.astype(BF16), vcg))
        sel_all.append(jnp.where(valid, _top_n(score, blk, NSA_TOPN), 0.0).astype(BF16))

    def selected_and_combine(n_keys):
        er = lax.broadcasted_iota(jnp.int32, (n_cmp, n_keys), 0)
        ec = lax.broadcasted_iota(jnp.int32, (n_cmp, n_keys), 1)
        expand = jnp.where(lax.shift_right_logical(ec, 6) == er, 1.0, 0.0).astype(BF16)
        for g in range(NSA_KV):
            sel_bias = jnp.where(_dot(sel_all[g], expand) > 0.5, 0.0, NEG)
            ksg = ks_ref[0:n_keys, g * HEAD_DIM:(g + 1) * HEAD_DIM].astype(BF16)
            vsg = vs_ref[0:n_keys, g * HEAD_DIM:(g + 1) * HEAD_DIM].astype(BF16)
            for r in range(NSA_REP):
                h = g * NSA_REP + r
                pieces = []
                for c in range(n_keys // ATT_BLOCK):
                    delta = qi - c
                    far = jnp.where(delta >= 2, far_ref[h:h + 1, 0:1], NEG)
                    pieces.append(jnp.where(delta == 0, bl_ref[h, :, ATT_BLOCK:2 * ATT_BLOCK],
                                            jnp.where(delta == 1, bl_ref[h, :, 0:ATT_BLOCK], far)))
                ls = _dot_nt(qh_all[g][r], ksg) + jnp.concatenate(pieces, axis=1) + sel_bias
                ps = jnp.exp(ls - jnp.max(ls, axis=-1, keepdims=True))
                den = jnp.sum(ps, axis=-1, keepdims=True)
                o_sel = _dot(ps.astype(BF16), vsg) / den
                out = (gates[:, 3 * h:3 * h + 1] * o_cmp_all[g][r * ATT_BLOCK:(r + 1) * ATT_BLOCK]
                       + gates[:, 3 * h + 1:3 * h + 2] * o_sel
                       + gates[:, 3 * h + 2:3 * h + 3] * win_ref[:, h * HEAD_DIM:(h + 1) * HEAD_DIM])
                o_ref[:, h * HEAD_DIM:(h + 1) * HEAD_DIM] = out.astype(BF16)

    tiles_per_span = NSA_KEY_SPAN // ATT_BLOCK
    for c in range(seq // NSA_KEY_SPAN):
        pl.when(qi // tiles_per_span == c)(functools.partial(selected_and_combine, (c + 1) * NSA_KEY_SPAN))


def nsa_prompt_attention(proj, o_win, kc_blk, vc_blk, table, batch, seq):
    nq = seq // ATT_BLOCK
    n_cmp = seq // NSA_BLOCK
    assert n_cmp >= NSA_TOPN and seq % NSA_KEY_SPAN == 0
    dist = jnp.arange(seq)[:, None] - (jnp.arange(n_cmp) * NSA_BLOCK + NSA_BLOCK - 1)[None, :]
    bias_c = jnp.where(dist >= 0, head_bias(table, dist), NEG)
    bias_c = bias_c.reshape(NSA_KV, NSA_REP, nq, ATT_BLOCK, n_cmp).transpose(0, 2, 1, 3, 4)
    bias_c = bias_c.reshape(NSA_KV, nq, NSA_REP * ATT_BLOCK, n_cmp)
    bias_l = local_bias(table, 1, 1 << 30, ATT_HEADS).reshape(ATT_HEADS, ATT_BLOCK, 2 * ATT_BLOCK)
    far = jnp.broadcast_to(table[REL_BUCKETS - 1][:, None].astype(F32), (ATT_HEADS, LANE))
    full = lambda shape: pl.BlockSpec(shape, lambda b, i: (0,) * len(shape))
    return pl.pallas_call(
        _nsa_prompt_kernel,
        grid=(batch, nq),
        in_specs=[
            pl.BlockSpec((ATT_BLOCK, NSA_QD), lambda b, i: (b * nq + i, 0)),
            pl.BlockSpec((ATT_BLOCK, NSA_QD), lambda b, i: (b * nq + i, 0)),
            pl.BlockSpec((ATT_BLOCK, LANE), lambda b, i: (b * nq + i, NSA_GATE_BLK)),
            pl.BlockSpec((n_cmp, NSA_KVW), lambda b, i: (b, 0)),
            pl.BlockSpec((n_cmp, NSA_KVW), lambda b, i: (b, 0)),
            pl.BlockSpec((seq, NSA_KVW), lambda b, i: (b, NSA_QD // NSA_KVW + 2)),
            pl.BlockSpec((seq, NSA_KVW), lambda b, i: (b, NSA_QD // NSA_KVW + 3)),
            pl.BlockSpec((NSA_KV, 1, NSA_REP * ATT_BLOCK, n_cmp), lambda b, i: (0, i, 0, 0)),
            full(bias_l.shape), full(far.shape),
        ],
        out_specs=pl.BlockSpec((ATT_BLOCK, NSA_QD), lambda b, i: (b * nq + i, 0)),
        out_shape=jax.ShapeDtypeStruct((batch * seq, NSA_QD), BF16),
        compiler_params=_cparams("parallel", "arbitrary"),
        name="nsa_prompt_attention",
    )(proj, o_win, proj, kc_blk, vc_blk, proj, proj, bias_c, bias_l, far)


def _nsa_sample_cmp_kernel(q_ref, kc_ref, vc_ref, bias_ref, o_ref, idx_ref, *, cur, n_sel):
    n_cmp = kc_ref.shape[1]
    width = _round_up(n_sel, LANE)
    scale = HEAD_DIM ** -0.5
    lane = lax.broadcasted_iota(jnp.int32, (1, width), 1)
    lane_out = lax.broadcasted_iota(jnp.int32, (1, LANE), 1)
    valid = lane <= cur
    forced = valid & ((lane == 0) | (lane >= cur - 1))
    for g in range(NSA_KV):
        qg = jnp.concatenate([q_ref[0, :, (g * NSA_REP + r) * HEAD_DIM:(g * NSA_REP + r + 1) * HEAD_DIM]
                              for r in range(NSA_REP)], axis=0) * scale
        bias = bias_ref[g * NSA_REP:(g + 1) * NSA_REP, :]
        ok = bias > 0.5 * NEG
        lc = _dot_nt(qg.astype(BF16), kc_ref[0, :, g * HEAD_DIM:(g + 1) * HEAD_DIM].astype(BF16)) + bias
        p = jnp.where(ok, jnp.exp(lc - jnp.max(lc, axis=-1, keepdims=True)), 0.0)
        p = p / jnp.maximum(jnp.sum(p, axis=-1, keepdims=True), 1e-30)
        o_cmp = _dot(p.astype(BF16), vc_ref[0, :, g * HEAD_DIM:(g + 1) * HEAD_DIM].astype(BF16))
        for r in range(NSA_REP):
            h = g * NSA_REP + r
            o_ref[0, :, h * HEAD_DIM:(h + 1) * HEAD_DIM] = o_cmp[r:r + 1, :]
        p_slc = jnp.sum(p, axis=0, keepdims=True)
        score = jnp.concatenate([p_slc, jnp.zeros((1, width - n_cmp), F32)], axis=1)
        score = jnp.where(forced, NSA_FORCE, jnp.where(valid, score, -1.0))
        score = jnp.where(lane < n_sel, score, -jnp.inf)
        idx_row = jnp.zeros((1, LANE), jnp.int32)
        for it in range(NSA_TOPN):
            m = jnp.max(score, axis=-1, keepdims=True)
            j = jnp.min(jnp.where(score == m, lane, 1 << 30), axis=-1, keepdims=True)
            idx_row = jnp.where(lane_out == it, j, idx_row)
            score = jnp.where(lane == j, -jnp.inf, score)
        idx_ref[0, :, g * LANE:(g + 1) * LANE] = idx_row


def _nsa_sample_sel_kernel(idx_ref, pt_ref, q_ref, kn_ref, vn_ref, k0_ref, k1_ref, v0_ref, v1_ref, b0_ref, b1_ref,
                           o_ref, m_ref, l_ref, acc_ref, *, cur, n_past_blk):
    b = pl.program_id(0)
    n = pl.program_id(1)
    scale = HEAD_DIM ** -0.5

    @pl.when(n == 0)
    def _():
        m_ref[...] = jnp.full(m_ref.shape, NEG, F32)
        l_ref[...] = jnp.zeros_like(l_ref)
        acc_ref[...] = jnp.zeros_like(acc_ref)

    row = lax.broadcasted_iota(jnp.int32, (BLOCK_ROWS, HEAD_DIM), 0)
    key_head = lax.rem(lax.broadcasted_iota(jnp.int32, (1, BLOCK_ROWS), 1), NSA_KV)
    for g, (k_ref, v_ref, bias_ref) in enumerate(((k0_ref, v0_ref, b0_ref), (k1_ref, v1_ref, b1_ref))):
        i = idx_ref[(b * NSA_KV + g) * NSA_TOPN + n]
        is_new = i >= n_past_blk
        knew = jnp.where(row == g, kn_ref[0, :, g * HEAD_DIM:(g + 1) * HEAD_DIM], 0.0)
        vnew = jnp.where(row == g, vn_ref[0, :, g * HEAD_DIM:(g + 1) * HEAD_DIM], 0.0)
        kblk = jnp.where(is_new, knew, k_ref[...]).astype(BF16)
        vblk = jnp.where(is_new, vnew, v_ref[...]).astype(BF16)
        qg = jnp.concatenate([q_ref[0, :, (g * NSA_REP + r) * HEAD_DIM:(g * NSA_REP + r + 1) * HEAD_DIM]
                              for r in range(NSA_REP)], axis=0) * scale
        rows = slice(g * NSA_REP, (g + 1) * NSA_REP)
        bias = bias_ref[0, rows, :]
        ok = (bias > 0.5 * NEG) & (key_head == g) & (i <= cur)
        ls = jnp.where(ok, _dot_nt(qg.astype(BF16), kblk) + bias, NEG)
        m_old = m_ref[rows, :]
        m_new = jnp.maximum(m_old, jnp.max(ls, axis=-1, keepdims=True))
        alpha = jnp.exp(m_old - m_new)
        ps = jnp.where(ok, jnp.exp(ls - m_new[:, 0:1]), 0.0)
        l_ref[rows, :] = alpha * l_ref[rows, :] + jnp.sum(ps, axis=-1, keepdims=True)
        acc_ref[rows, :] = alpha * acc_ref[rows, :] + _dot(ps.astype(BF16), vblk)
        m_ref[rows, :] = m_new

    @pl.when(n == pl.num_programs(1) - 1)
    def _():
        out = acc_ref[...] / jnp.maximum(l_ref[...], 1e-30)
        for h in range(ATT_HEADS):
            o_ref[0, :, h * HEAD_DIM:(h + 1) * HEAD_DIM] = out[h:h + 1, :]


def _nsa_combine_kernel(gate_ref, c_ref, s_ref, w_ref, o_ref):
    gates = _sigmoid(gate_ref[...])
    for h in range(ATT_HEADS):
        cols = slice(h * HEAD_DIM, (h + 1) * HEAD_DIM)
        out = (gates[:, 3 * h:3 * h + 1] * c_ref[:, cols] + gates[:, 3 * h + 1:3 * h + 2] * s_ref[:, cols]
               + gates[:, 3 * h + 2:3 * h + 3] * w_ref[:, cols])
        o_ref[:, cols] = out.astype(BF16)


def nsa_sample_attention(proj, kc_blk, vc_blk, sel_k_pool, sel_v_pool, win_k, win_v, page_table, table, past):
    nb = proj.shape[0]
    n_pages = page_table.shape[1]
    n_cmp = kc_blk.shape[0] // nb
    sub = PAGE_SIZE // NSA_BLOCK
    n_past_blk = n_pages * sub
    cur = past // NSA_BLOCK
    n_sel = -(-(past + 1) // NSA_BLOCK)
    proj3 = proj.reshape(nb, 1, -1)
    full = lambda shape: pl.BlockSpec(shape, lambda *_: (0,) * len(shape))
    dist_c = past - (jnp.arange(n_cmp) * NSA_BLOCK + NSA_BLOCK - 1)
    bias_c = jnp.where(dist_c >= 0, head_bias(table, dist_c), NEG)
    o_cmp, idx = pl.pallas_call(
        functools.partial(_nsa_sample_cmp_kernel, cur=cur, n_sel=n_sel),
        grid=(nb,),
        in_specs=[
            pl.BlockSpec((1, 1, NSA_QD), lambda b: (b, 0, 0)),
            pl.BlockSpec((1, n_cmp, NSA_KVW), lambda b: (b, 0, 0)),
            pl.BlockSpec((1, n_cmp, NSA_KVW), lambda b: (b, 0, 0)),
            full(bias_c.shape),
        ],
        out_specs=[pl.BlockSpec((1, 1, NSA_QD), lambda b: (b, 0, 0)), pl.BlockSpec((1, 1, NSA_KV * LANE), lambda b: (b, 0, 0))],
        out_shape=[jax.ShapeDtypeStruct((nb, 1, NSA_QD), F32), jax.ShapeDtypeStruct((nb, 1, NSA_KV * LANE), jnp.int32)],
        compiler_params=_cparams("parallel"),
        name="nsa_sample_cmp",
    )(proj3, kc_blk.reshape(nb, n_cmp, NSA_KVW), vc_blk.reshape(nb, n_cmp, NSA_KVW), bias_c)
    idx_flat = idx.reshape(nb, NSA_KV, LANE)[:, :, :NSA_TOPN].reshape(-1)

    dist_s = past - jnp.arange(n_sel * NSA_BLOCK)
    bias_s = jnp.where(dist_s >= 0, head_bias(table, dist_s), NEG).reshape(ATT_HEADS, n_sel, NSA_BLOCK)
    bias_s = jnp.repeat(jnp.swapaxes(bias_s, 0, 1), NSA_KV, axis=2)

    def phys(b, n, g, idx_ref, pt_ref):
        i = jnp.minimum(idx_ref[(b * NSA_KV + g) * NSA_TOPN + n], n_past_blk - 1)
        return pt_ref[b, i // sub] * sub + i % sub

    pool_spec = lambda g: pl.BlockSpec((BLOCK_ROWS, HEAD_DIM), lambda b, n, idx_ref, pt_ref: (phys(b, n, g, idx_ref, pt_ref), 0))
    bias_spec = lambda g: pl.BlockSpec(
        (1, ATT_HEADS, BLOCK_ROWS), lambda b, n, idx_ref, pt_ref: (idx_ref[(b * NSA_KV + g) * NSA_TOPN + n], 0, 0))
    row_spec = lambda width, blk: pl.BlockSpec((1, 1, width), lambda b, n, idx_ref, pt_ref: (b, 0, blk))
    pool_k = sel_k_pool.reshape(-1, HEAD_DIM)
    pool_v = sel_v_pool.reshape(-1, HEAD_DIM)
    o_sel = pl.pallas_call(
        functools.partial(_nsa_sample_sel_kernel, cur=cur, n_past_blk=n_past_blk),
        grid_spec=pltpu.PrefetchScalarGridSpec(
            num_scalar_prefetch=2,
            grid=(nb, NSA_TOPN),
            in_specs=[row_spec(NSA_QD, 0), row_spec(NSA_KVW, NSA_QD // NSA_KVW + 2), row_spec(NSA_KVW, NSA_QD // NSA_KVW + 3),
                      pool_spec(0), pool_spec(1), pool_spec(0), pool_spec(1), bias_spec(0), bias_spec(1)],
            out_specs=row_spec(NSA_QD, 0),
            scratch_shapes=[pltpu.VMEM((ATT_HEADS, LANE), F32), pltpu.VMEM((ATT_HEADS, LANE), F32),
                            pltpu.VMEM((ATT_HEADS, HEAD_DIM), F32)],
        ),
        out_shape=jax.ShapeDtypeStruct((nb, 1, NSA_QD), F32),
        compiler_params=_cparams("parallel", "arbitrary"),
        name="nsa_sample_sel",
    )(idx_flat, page_table, proj3, proj3, proj3, pool_k, pool_k, pool_v, pool_v, bias_s, bias_s)

    o_win = decode_attention(proj, 0, NSA_QD // NSA_KVW + 4, NSA_QD // NSA_KVW + 5, win_k, win_v, NSA_KV, table, None, F32)
    row2 = lambda width, blk: pl.BlockSpec((nb, width), lambda i: (0, blk))
    return pl.pallas_call(
        _nsa_combine_kernel,
        grid=(1,),
        in_specs=[row2(LANE, NSA_GATE_BLK), row2(NSA_QD, 0), row2(NSA_QD, 0), row2(NSA_QD, 0)],
        out_specs=row2(NSA_QD, 0),
        out_shape=jax.ShapeDtypeStruct((nb, NSA_QD), BF16),
        compiler_params=_cparams("arbitrary"),
        name="nsa_combine",
    )(proj, o_cmp.reshape(nb, NSA_QD), o_sel.reshape(nb, NSA_QD), o_win)


def nsa_weights(w_in):
    return _pad_cols(w_in, _round_up(w_in.shape[1], COL_TILE)).astype(BF16)


def nsa_layer(xp, xs, batch, seq, past, norm_g_mix, table, w_in, pos_k, pos_v, w1_k, w2_k, w1_v, w2_v, w_o,
              cmp_k_pool, cmp_v_pool, sel_k_pool, sel_v_pool, win_k, win_v, page_table):
    w_in_b = nsa_weights(w_in)
    w_o_b = w_o.astype(BF16)
    kv0 = NSA_QD // NSA_KVW
    nb = xs.shape[0]

    proj_p = norm_proj(xp, norm_g_mix, w_in_b)
    kc_blk = compress_rows(proj_p, kv0, pos_k, w1_k, w2_k)
    vc_blk = compress_rows(proj_p, kv0 + 1, pos_v, w1_v, w2_v)
    n_prev = -(-NSA_WINDOW // ATT_BLOCK)
    bias_w = local_bias(table, n_prev, NSA_WINDOW, NSA_KV)
    o_win = banded_attention(proj_p, batch, seq, 0, kv0 + 4, kv0 + 5, NSA_KV, n_prev, bias_w, None, F32)
    o_p = nsa_prompt_attention(proj_p, o_win, kc_blk, vc_blk, table, batch, seq)
    xp = proj_res(o_p, w_o_b, xp)
    kvs_p = proj_p.reshape(batch, seq, -1)[:, :, NSA_QD:NSA_QD + 6 * NSA_KVW].reshape(batch, seq, 6, NSA_KV, HEAD_DIM)
    buf = min(NSA_WINDOW, seq)
    outs_p = [kvs_p[:, :, j] for j in range(4)] + [kvs_p[:, seq - buf:, 4], kvs_p[:, seq - buf:, 5]]

    proj_s = norm_proj(xs, norm_g_mix, w_in_b)
    kc_s = compress_paged(cmp_k_pool, page_table, pos_k, w1_k, w2_k)
    vc_s = compress_paged(cmp_v_pool, page_table, pos_v, w1_v, w2_v)
    o_s = nsa_sample_attention(proj_s, kc_s, vc_s, sel_k_pool, sel_v_pool, win_k, win_v, page_table, table, past)
    xs = proj_res(o_s, w_o_b, xs)
    kvs_s = proj_s[:, NSA_QD:NSA_QD + 6 * NSA_KVW].reshape(nb, 1, 6, NSA_KV, HEAD_DIM)
    outs_s = [kvs_s[:, :, j] for j in range(4)]
    outs_s += [jnp.concatenate([win_k[:, 1:], kvs_s[:, :, 4]], axis=1), jnp.concatenate([win_v[:, 1:], kvs_s[:, :, 5]], axis=1)]
    return xp, xs, outs_p, outs_s


def kernel(x_prompt, x_sample, state_ssd_conv, state_ssd, cache_swa_k, cache_swa_v, state_gdn_conv, state_gdn,
           cache_nsa_cmp_k, cache_nsa_cmp_v, cache_nsa_sel_k, cache_nsa_sel_v, cache_nsa_win_k, cache_nsa_win_v,
           page_table, rel_table, norm_ffn1, norm_mix, norm_ffn2, norm_final,
           ffn1_gate, ffn1_up, ffn1_down, ffn2_gate, ffn2_up, ffn2_down,
           ssd_w_in, ssd_conv_w, ssd_conv_b, ssd_dt_bias, ssd_a_log, ssd_d, ssd_norm, ssd_w_out,
           swa_w_qkv, swa_sink, swa_w_o,
           gdn_w_in, gdn_conv_w, gdn_dt_bias, gdn_a_log, gdn_norm, gdn_w_out,
           nsa_w_in, nsa_pos_k, nsa_pos_v, nsa_cmp_w1_k, nsa_cmp_w2_k, nsa_cmp_w1_v, nsa_cmp_w2_v, nsa_w_o):
    batch, seq, d = x_prompt.shape
    nb = x_sample.shape[0]
    depth = norm_mix.shape[0]
    past = page_table.shape[1] * PAGE_SIZE
    xp = x_prompt.reshape(batch * seq, d)
    xs = x_sample.reshape(nb, d)
    outs_p = {}
    outs_s = {}

    def put(store, name, value):
        store.setdefault(name, []).append(value)

    w1 = ffn_weights(ffn1_gate, ffn1_up, ffn1_down)
    w2 = ffn_weights(ffn2_gate, ffn2_up, ffn2_down)
    for i in range(depth):
        kind, li = i % 4, i // 4
        xp = ffn(xp, norm_ffn1[i], *w1, i)
        xs = ffn(xs, norm_ffn1[i], *w1, i)
        if kind == 0:
            xp, xs, cp, sp, cs, ss = ssd_layer(
                xp, xs, batch, seq, norm_mix[i], ssd_w_in[li], ssd_conv_w[li], ssd_conv_b[li], ssd_dt_bias[li],
                ssd_a_log[li], ssd_d[li], ssd_norm[li], ssd_w_out[li], state_ssd_conv[li], state_ssd[li])
            put(outs_p, "ssd_conv", cp), put(outs_p, "ssd_state", sp)
            put(outs_s, "ssd_conv", cs), put(outs_s, "ssd_state", ss)
        elif kind == 1:
            xp, xs, kp, vp, kq, vq = swa_layer(xp, xs, batch, seq, norm_mix[i], rel_table, swa_w_qkv[li], swa_sink[li],
                                               swa_w_o[li], cache_swa_k[li], cache_swa_v[li])
            put(outs_p, "swa_k", kp), put(outs_p, "swa_v", vp)
            put(outs_s, "swa_k", kq), put(outs_s, "swa_v", vq)
        elif kind == 2:
            xp, xs, cp, sp, cs, ss = gdn_layer(
                xp, xs, batch, seq, norm_mix[i], gdn_w_in[li], gdn_conv_w[li], gdn_dt_bias[li], gdn_a_log[li],
                gdn_norm[li], gdn_w_out[li], state_gdn_conv[li], state_gdn[li])
            put(outs_p, "gdn_conv", cp), put(outs_p, "gdn_state", sp)
            put(outs_s, "gdn_conv", cs), put(outs_s, "gdn_state", ss)
        else:
            xp, xs, op, os_ = nsa_layer(
                xp, xs, batch, seq, past, norm_mix[i], rel_table, nsa_w_in[li], nsa_pos_k[li], nsa_pos_v[li],
                nsa_cmp_w1_k[li], nsa_cmp_w2_k[li], nsa_cmp_w1_v[li], nsa_cmp_w2_v[li], nsa_w_o[li],
                cache_nsa_cmp_k[li], cache_nsa_cmp_v[li], cache_nsa_sel_k[li], cache_nsa_sel_v[li],
                cache_nsa_win_k[li], cache_nsa_win_v[li], page_table)
            for j in range(6):
                put(outs_p, f"nsa{j}", op[j]), put(outs_s, f"nsa{j}", os_[j])
        xp = ffn(xp, norm_ffn2[i], *w2, i)
        xs = ffn(xs, norm_ffn2[i], *w2, i)

    y_prompt = final_norm(xp, norm_final).reshape(batch, seq, d)
    y_sample = final_norm(xs, norm_final).reshape(nb, 1, d)
    order = ["ssd_conv", "ssd_state", "swa_k", "swa_v", "gdn_conv", "gdn_state"] + [f"nsa{j}" for j in range(6)]
    states_p = tuple(jnp.stack(outs_p[name]) for name in order)
    states_s = tuple(jnp.stack(outs_s[name]) for name in order)
    return (y_prompt, y_sample) + states_p + states_s
```

```python
import functools
import math

import jax
import jax.numpy as jnp
from jax import lax
from jax.experimental import pallas as pl
from jax.experimental.pallas import tpu as pltpu

F32 = jnp.float32
BF16 = jnp.bfloat16
HIGHEST = lax.Precision.HIGHEST

EPS = 1e-6
D_MODEL = 2048
D_FF = 5504
LANE = 128
VMEM_LIMIT = 56 * 1024 * 1024
NEG = -1e30

FF_TILE = 512
D_FF_PAD = -(-D_FF // FF_TILE) * FF_TILE
ROW_TILE = 1024
FFN_ROW_TILE = 1024
COL_TILE = 512

SSD_INNER = 4096
SSD_HEADS = 64
SSD_P = 64
SSD_N = 128
SSD_GROUPS = 8
SSD_GW = SSD_INNER // SSD_GROUPS
SSD_CONV_DIM = SSD_INNER + 2 * SSD_GROUPS * SSD_N
SSD_CHUNK = 128


def _cparams(*sem):
    return pltpu.CompilerParams(dimension_semantics=sem, vmem_limit_bytes=VMEM_LIMIT)


def _pad_cols(w, n):
    return jnp.pad(w, ((0, 0), (0, n - w.shape[1])))


def _round_up(n, m):
    return -(-n // m) * m


def _sigmoid(x):
    return 1.0 / (1.0 + jnp.exp(-x))


def _silu(x):
    return x * _sigmoid(x)


def _softplus(x):
    return jnp.maximum(x, 0.0) + jnp.log(1.0 + jnp.exp(-jnp.abs(x)))


def _rms(x, g):
    return x * lax.rsqrt(jnp.mean(x * x, axis=-1, keepdims=True) + EPS) * g


def _dot(a, b, **kw):
    return jnp.dot(a, b, preferred_element_type=F32, **kw)


def _dot_nt(a, b):
    return lax.dot_general(a, b, (((1,), (1,)), ((), ())), preferred_element_type=F32)


def _dot_tn(a, b):
    return lax.dot_general(a, b, (((0,), (0,)), ((), ())), preferred_element_type=F32)


def _eye(n):
    r = lax.broadcasted_iota(jnp.int32, (n, n), 0)
    c = lax.broadcasted_iota(jnp.int32, (n, n), 1)
    return r == c


def _row_to_col(v, eye):
    n = v.shape[1]
    return jnp.sum(jnp.where(eye, jnp.broadcast_to(v, (n, n)), 0.0), axis=-1, keepdims=True)


def _col_to_row(v, eye):
    n = v.shape[0]
    return jnp.sum(jnp.where(eye, jnp.broadcast_to(v, (n, n)), 0.0), axis=0, keepdims=True)


def _ffn_kernel(x_ref, g_ref, wg_ref, wu_ref, wd_ref, o_ref, xn_ref):
    @pl.when(pl.program_id(1) == 0)
    def _():
        x = x_ref[...]
        xn_ref[...] = _rms(x, g_ref[...]).astype(BF16)
        o_ref[...] = x

    xn = xn_ref[...]
    gate = _dot(xn, wg_ref[...])
    up = _dot(xn, wu_ref[...])
    h = (_silu(gate) * (0.5 * up)).astype(BF16)
    o_ref[...] += _dot(h, wd_ref[...])


def ffn(x, g, wg, wu, wd, layer):
    m, d = x.shape
    tm = min(FFN_ROW_TILE, m)
    return pl.pallas_call(
        _ffn_kernel,
        grid=(m // tm, D_FF_PAD // FF_TILE),
        in_specs=[
            pl.BlockSpec((tm, d), lambda i, f: (i, 0), pipeline_mode=pl.Buffered(1)),
            pl.BlockSpec((1, d), lambda i, f: (0, 0)),
            pl.BlockSpec((None, d, FF_TILE), lambda i, f: (layer, 0, f)),
            pl.BlockSpec((None, d, FF_TILE), lambda i, f: (layer, 0, f)),
            pl.BlockSpec((None, FF_TILE, d), lambda i, f: (layer, f, 0)),
        ],
        out_specs=pl.BlockSpec((tm, d), lambda i, f: (i, 0)),
        out_shape=jax.ShapeDtypeStruct((m, d), F32),
        scratch_shapes=[pltpu.VMEM((tm, d), BF16)],
        compiler_params=_cparams("parallel", "arbitrary"),
        name="ffn",
    )(x, g.reshape(1, d), wg, wu, wd)


def _norm_proj_kernel(x_ref, g_ref, w_ref, o_ref, xn_ref):
    @pl.when(pl.program_id(1) == 0)
    def _():
        xn_ref[...] = _rms(x_ref[...], g_ref[...]).astype(BF16)

    o_ref[...] = _dot(xn_ref[...], w_ref[...])


def norm_proj(x, g, w):
    m, d = x.shape
    n = w.shape[1]
    tm = min(ROW_TILE, m)
    return pl.pallas_call(
        _norm_proj_kernel,
        grid=(m // tm, n // COL_TILE),
        in_specs=[
            pl.BlockSpec((tm, d), lambda i, j: (i, 0)),
            pl.BlockSpec((1, d), lambda i, j: (0, 0)),
            pl.BlockSpec((d, COL_TILE), lambda i, j: (0, j)),
        ],
        out_specs=pl.BlockSpec((tm, COL_TILE), lambda i, j: (i, j)),
        out_shape=jax.ShapeDtypeStruct((m, n), F32),
        scratch_shapes=[pltpu.VMEM((tm, d), BF16)],
        compiler_params=_cparams("parallel", "arbitrary"),
        name="norm_proj",
    )(x, g.reshape(1, d), w)


def _proj_res_kernel(a_ref, w_ref, r_ref, o_ref):
    o_ref[...] = r_ref[...] + _dot(a_ref[...], w_ref[...])


def proj_res(a, w, res):
    m, k = a.shape
    n = w.shape[1]
    tm = min(ROW_TILE, m)
    return pl.pallas_call(
        _proj_res_kernel,
        grid=(m // tm, n // COL_TILE),
        in_specs=[
            pl.BlockSpec((tm, k), lambda i, j: (i, 0)),
            pl.BlockSpec((k, COL_TILE), lambda i, j: (0, j)),
            pl.BlockSpec((tm, COL_TILE), lambda i, j: (i, j)),
        ],
        out_specs=pl.BlockSpec((tm, COL_TILE), lambda i, j: (i, j)),
        out_shape=jax.ShapeDtypeStruct((m, n), F32),
        compiler_params=_cparams("parallel", "parallel"),
        name="proj_res",
    )(a, w, res)


def _final_norm_kernel(x_ref, g_ref, o_ref):
    o_ref[...] = _rms(x_ref[...], g_ref[...])


def final_norm(x, g):
    m, d = x.shape
    tm = min(ROW_TILE, m)
    return pl.pallas_call(
        _final_norm_kernel,
        grid=(m // tm,),
        in_specs=[pl.BlockSpec((tm, d), lambda i: (i, 0)), pl.BlockSpec((1, d), lambda i: (0, 0))],
        out_specs=pl.BlockSpec((tm, d), lambda i: (i, 0)),
        out_shape=jax.ShapeDtypeStruct((m, d), F32),
        compiler_params=_cparams("parallel"),
        name="final_norm",
    )(x, g.reshape(1, d))


def _head_expand(width):
    r = lax.broadcasted_iota(jnp.int32, (LANE, width), 0)
    c = lax.broadcasted_iota(jnp.int32, (LANE, width), 1)
    return jnp.where(lax.shift_right_logical(c, 6) == r, 1.0, 0.0).astype(F32)


def _ssd_prompt_kernel(z_ref, x_ref, bc_ref, dt_ref, cw_ref, cb_ref, dtb_ref, alog_ref, dx_ref, ng_ref,
                       y_ref, st_ref, xpad_ref, s_ref, ybuf_ref):
    c = pl.program_id(1)
    L = SSD_CHUNK

    @pl.when(c == 0)
    def _():
        xpad_ref[0:8, :] = jnp.zeros((8, SSD_CONV_DIM), F32)
        s_ref[...] = jnp.zeros_like(s_ref)

    xpad_ref[8:8 + L, 0:SSD_INNER] = x_ref[...]
    xpad_ref[8:8 + L, SSD_INNER:SSD_CONV_DIM] = bc_ref[...]
    conv = cb_ref[...] + cw_ref[3:4, :] * xpad_ref[8:8 + L, :]
    for j in range(3):
        conv = conv + cw_ref[j:j + 1, :] * xpad_ref[pl.ds(5 + j, L), :]
    xpad_ref[0:8, :] = xpad_ref[L:L + 8, :]
    act = _silu(conv)
    xs = act[:, 0:SSD_INNER]
    bm = act[:, SSD_INNER:SSD_INNER + SSD_GROUPS * SSD_N]
    cm = act[:, SSD_INNER + SSD_GROUPS * SSD_N:SSD_CONV_DIM]

    dt = _softplus(dt_ref[...] + dtb_ref[...])
    da = dt * (-jnp.exp(alog_ref[...]))
    row = lax.broadcasted_iota(jnp.int32, (L, L), 0)
    col = lax.broadcasted_iota(jnp.int32, (L, L), 1)
    causal = row >= col
    cs = _dot(jnp.where(causal, 1.0, 0.0).astype(F32), da, precision=HIGHEST)
    cs_t = cs.T
    expand = _head_expand(SSD_INNER)
    dtx = _dot(dt, expand, precision=HIGHEST)
    csx = _dot(cs, expand, precision=HIGHEST)
    xdt = xs * dtx
    cs_last = csx[L - 1:L, :]
    xdt_end = (xdt * jnp.exp(cs_last - csx)).astype(BF16)
    xdt_b = xdt.astype(BF16)
    ecs = jnp.exp(csx)
    dec_last = jnp.exp(cs_last)
    lane = lax.broadcasted_iota(jnp.int32, (L, LANE), 1)

    for g in range(SSD_GROUPS):
        lo_n = g * SSD_N
        lo_f = g * SSD_GW
        bg = bm[:, lo_n:lo_n + SSD_N]
        cg = cm[:, lo_n:lo_n + SSD_N].astype(BF16)
        cb = _dot_nt(cg, bg.astype(BF16))
        sg = s_ref[:, lo_f:lo_f + SSD_GW]
        yoff = _dot(cg, sg.astype(BF16)) * ecs[:, lo_f:lo_f + SSD_GW]
        for pr in range(SSD_GW // LANE):
            lo = lo_f + pr * LANE
            xp = xdt_b[:, lo:lo + LANE]
            outs = []
            for hh in (g * 8 + 2 * pr, g * 8 + 2 * pr + 1):
                diff = cs[:, hh:hh + 1] - cs_t[hh:hh + 1, :]
                dec = jnp.exp(jnp.where(causal, diff, NEG))
                outs.append(_dot((cb * dec).astype(BF16), xp))
            ybuf_ref[:, lo:lo + LANE] = jnp.where(lane < SSD_P, outs[0], outs[1]) + yoff[:, pr * LANE:(pr + 1) * LANE]
        s_ref[:, lo_f:lo_f + SSD_GW] = (sg * dec_last[:, lo_f:lo_f + SSD_GW]
                                        + _dot(bg.T.astype(BF16), xdt_end[:, lo_f:lo_f + SSD_GW]))

    y = ybuf_ref[...] + xs * dx_ref[...]
    yz = y * _silu(z_ref[...])
    for g in range(SSD_GROUPS):
        lo_f = g * SSD_GW
        y_ref[:, lo_f:lo_f + SSD_GW] = _rms(yz[:, lo_f:lo_f + SSD_GW], ng_ref[:, lo_f:lo_f + SSD_GW]).astype(BF16)

    @pl.when(c == pl.num_programs(1) - 1)
    def _():
        for k in range(SSD_INNER // LANE):
            st_ref[0, k * LANE:(k + 1) * LANE, :] = s_ref[:, k * LANE:(k + 1) * LANE].T


def _pad_lanes(v, n=LANE):
    v = v.reshape(1, -1).astype(F32)
    return jnp.pad(v, ((0, 0), (0, n - v.shape[1])))


def ssd_prompt(proj, batch, seq, conv_w, conv_b, dt_bias, a_log, d_skip, norm_g):
    L = SSD_CHUNK
    nc = seq // L
    rowb = lambda b, c: b * nc + c
    full = lambda shape: pl.BlockSpec(shape, lambda b, c: (0,) * len(shape))
    y, st = pl.pallas_call(
        _ssd_prompt_kernel,
        grid=(batch, nc),
        in_specs=[
            pl.BlockSpec((L, SSD_INNER), lambda b, c: (rowb(b, c), 0)),
            pl.BlockSpec((L, SSD_INNER), lambda b, c: (rowb(b, c), 1)),
            pl.BlockSpec((L, 2 * SSD_GROUPS * SSD_N), lambda b, c: (rowb(b, c), 4)),
            pl.BlockSpec((L, LANE), lambda b, c: (rowb(b, c), (SSD_INNER + SSD_CONV_DIM) // LANE)),
            full((4, SSD_CONV_DIM)), full((1, SSD_CONV_DIM)), full((1, LANE)), full((1, LANE)),
            full((1, SSD_INNER)), full((1, SSD_INNER)),
        ],
        out_specs=[
            pl.BlockSpec((L, SSD_INNER), lambda b, c: (rowb(b, c), 0)),
            pl.BlockSpec((1, SSD_INNER, SSD_N), lambda b, c: (b, 0, 0)),
        ],
        out_shape=[
            jax.ShapeDtypeStruct((batch * seq, SSD_INNER), BF16),
            jax.ShapeDtypeStruct((batch, SSD_INNER, SSD_N), F32),
        ],
        scratch_shapes=[
            pltpu.VMEM((L + 8, SSD_CONV_DIM), F32),
            pltpu.VMEM((SSD_N, SSD_INNER), F32),
            pltpu.VMEM((L, SSD_INNER), F32),
        ],
        compiler_params=_cparams("parallel", "arbitrary"),
        name="ssd_prompt",
    )(proj, proj, proj, proj, conv_w, conv_b.reshape(1, -1), _pad_lanes(dt_bias), _pad_lanes(a_log),
      jnp.repeat(d_skip, SSD_P).reshape(1, -1), norm_g.reshape(1, -1))
    return y, st.reshape(batch, SSD_HEADS, SSD_P, SSD_N)


def _ssd_sample_pre_kernel(x_ref, bc_ref, dt_ref, prev_ref, cw_ref, cb_ref, dtb_ref, alog_ref,
                           xs_ref, b_ref, c_ref, xdt_ref, dec_ref):
    xnew = jnp.concatenate([x_ref[...], bc_ref[...]], axis=1)
    conv = cb_ref[...] + cw_ref[3:4, :] * xnew
    for j in range(3):
        conv = conv + cw_ref[j:j + 1, :] * prev_ref[j]
    act = _silu(conv)
    xs = act[:, 0:SSD_INNER]
    dt = _softplus(dt_ref[...] + dtb_ref[...])
    da = dt * (-jnp.exp(alog_ref[...]))
    expand = _head_expand(SSD_INNER)
    xs_ref[...] = xs
    b_ref[...] = act[:, SSD_INNER:SSD_INNER + SSD_GROUPS * SSD_N]
    c_ref[...] = act[:, SSD_INNER + SSD_GROUPS * SSD_N:SSD_CONV_DIM]
    xdt_ref[...] = xs * _dot(dt, expand, precision=HIGHEST)
    dec_ref[...] = jnp.exp(_dot(da, expand, precision=HIGHEST))


def _ssd_sample_state_kernel(s_ref, xdt_ref, dec_ref, b_ref, c_ref, so_ref, y_ref):
    eye = _eye(LANE)
    for k in range(SSD_INNER // LANE):
        g = k // (SSD_GW // LANE)
        lo = k * LANE
        xcol = _row_to_col(xdt_ref[0, :, lo:lo + LANE], eye)
        dcol = _row_to_col(dec_ref[0, :, lo:lo + LANE], eye)
        brow = b_ref[0, :, g * SSD_N:(g + 1) * SSD_N]
        crow = c_ref[0, :, g * SSD_N:(g + 1) * SSD_N]
        snew = dcol * s_ref[0, lo:lo + LANE, :] + xcol * brow
        so_ref[0, lo:lo + LANE, :] = snew
        ycol = jnp.sum(snew * crow, axis=-1, keepdims=True)
        y_ref[0, :, lo:lo + LANE] = _col_to_row(ycol, eye)


def _ssd_sample_post_kernel(y_ref, xs_ref, z_ref, dx_ref, ng_ref, o_ref):
    y = y_ref[...] + xs_ref[...] * dx_ref[...]
    yz = y * _silu(z_ref[...])
    for g in range(SSD_GROUPS):
        lo_f = g * SSD_GW
        o_ref[:, lo_f:lo_f + SSD_GW] = _rms(yz[:, lo_f:lo_f + SSD_GW], ng_ref[:, lo_f:lo_f + SSD_GW]).astype(BF16)


def ssd_sample(proj, conv_prev, state, conv_w, conv_b, dt_bias, a_log, d_skip, norm_g):
    nb = proj.shape[0]
    full = lambda shape: pl.BlockSpec(shape, lambda *_: (0,) * len(shape))
    row_spec = lambda width, blk: pl.BlockSpec((nb, width), lambda *_: (0, blk))
    sds = lambda width: jax.ShapeDtypeStruct((nb, width), F32)
    gn = SSD_GROUPS * SSD_N
    xs, bm, cm, xdt, dec = pl.pallas_call(
        _ssd_sample_pre_kernel,
        grid=(1,),
        in_specs=[
            row_spec(SSD_INNER, 1), row_spec(2 * gn, 4), row_spec(LANE, (SSD_INNER + SSD_CONV_DIM) // LANE),
            full((3, nb, SSD_CONV_DIM)), full((4, SSD_CONV_DIM)), full((1, SSD_CONV_DIM)), full((1, LANE)), full((1, LANE)),
        ],
        out_specs=[full((nb, SSD_INNER)), full((nb, gn)), full((nb, gn)), full((nb, SSD_INNER)), full((nb, SSD_INNER))],
        out_shape=[sds(SSD_INNER), sds(gn), sds(gn), sds(SSD_INNER), sds(SSD_INNER)],
        compiler_params=_cparams("arbitrary"),
        name="ssd_sample_pre",
    )(proj, proj, proj, jnp.swapaxes(conv_prev, 0, 1), conv_w, conv_b.reshape(1, -1), _pad_lanes(dt_bias), _pad_lanes(a_log))
    per_row = lambda width: pl.BlockSpec((1, 1, width), lambda b: (b, 0, 0))
    s_new, y = pl.pallas_call(
        _ssd_sample_state_kernel,
        grid=(nb,),
        in_specs=[
            pl.BlockSpec((1, SSD_INNER, SSD_N), lambda b: (b, 0, 0)),
            per_row(SSD_INNER), per_row(SSD_INNER), per_row(gn), per_row(gn),
        ],
        out_specs=[pl.BlockSpec((1, SSD_INNER, SSD_N), lambda b: (b, 0, 0)), per_row(SSD_INNER)],
        out_shape=[jax.ShapeDtypeStruct((nb, SSD_INNER, SSD_N), F32), jax.ShapeDtypeStruct((nb, 1, SSD_INNER), F32)],
        compiler_params=_cparams("parallel"),
        name="ssd_sample_state",
    )(state.reshape(nb, SSD_INNER, SSD_N), xdt.reshape(nb, 1, -1), dec.reshape(nb, 1, -1),
      bm.reshape(nb, 1, -1), cm.reshape(nb, 1, -1))
    y = y.reshape(nb, SSD_INNER)
    out = pl.pallas_call(
        _ssd_sample_post_kernel,
        grid=(1,),
        in_specs=[full((nb, SSD_INNER)), full((nb, SSD_INNER)), row_spec(SSD_INNER, 0), full((1, SSD_INNER)), full((1, SSD_INNER))],
        out_specs=full((nb, SSD_INNER)),
        out_shape=jax.ShapeDtypeStruct((nb, SSD_INNER), BF16),
        compiler_params=_cparams("arbitrary"),
        name="ssd_sample_post",
    )(y, xs, proj, jnp.repeat(d_skip, SSD_P).reshape(1, -1), norm_g.reshape(1, -1))
    return out, s_new.reshape(nb, SSD_HEADS, SSD_P, SSD_N)


def ssd_weights(w_in):
    n = _round_up(w_in.shape[1], COL_TILE)
    return _pad_cols(w_in, n).astype(BF16)


def ssd_layer(xp, xs, batch, seq, norm_g_mix, w_in, conv_w, conv_b, dt_bias, a_log, d_skip, norm_g, w_out,
              conv_state, ssm_state):
    w_in_b = ssd_weights(w_in)
    w_out_b = w_out.astype(BF16)
    args = (conv_w, conv_b, dt_bias, a_log, d_skip, norm_g)
    proj_p = norm_proj(xp, norm_g_mix, w_in_b)
    y_p, st_p = ssd_prompt(proj_p, batch, seq, *args)
    xp = proj_res(y_p, w_out_b, xp)
    raw_p = proj_p.reshape(batch, seq, -1)[:, seq - 3:, SSD_INNER:SSD_INNER + SSD_CONV_DIM]
    proj_s = norm_proj(xs, norm_g_mix, w_in_b)
    y_s, st_s = ssd_sample(proj_s, conv_state, ssm_state, *args)
    xs = proj_res(y_s, w_out_b, xs)
    raw_s = jnp.concatenate([conv_state[:, 1:], proj_s[:, None, SSD_INNER:SSD_INNER + SSD_CONV_DIM]], axis=1)
    return xp, xs, raw_p, st_p, raw_s, st_s


CAST_ROWS = 128


def _cast_pad_kernel(x_ref, o_ref, *, row_tiles):
    o_ref[...] = jnp.zeros_like(o_ref)

    @pl.when(pl.program_id(1) < row_tiles)
    def _():
        o_ref[:, 0:x_ref.shape[1]] = x_ref[...].astype(BF16)


def cast_pad(w, rows, cols):
    depth, r, c = w.shape
    tile = 4 * CAST_ROWS if rows == r else CAST_ROWS
    row_tiles = r // tile
    return pl.pallas_call(
        functools.partial(_cast_pad_kernel, row_tiles=row_tiles),
        grid=(depth, rows // tile),
        in_specs=[pl.BlockSpec((None, tile, c), lambda l, i: (l, jnp.minimum(i, row_tiles - 1), 0))],
        out_specs=pl.BlockSpec((None, tile, cols), lambda l, i: (l, i, 0)),
        out_shape=jax.ShapeDtypeStruct((depth, rows, cols), BF16),
        compiler_params=_cparams("parallel", "parallel"),
        name="cast_pad",
    )(w)


def ffn_weights(w_gate, w_up, w_down):
    d = w_gate.shape[1]
    return cast_pad(w_gate, d, D_FF_PAD), cast_pad(w_up, d, D_FF_PAD), cast_pad(w_down, D_FF_PAD, d)


HEAD_DIM = 128
ATT_HEADS = 16
ATT_BLOCK = 128
REL_BUCKETS = 32
REL_MAX_DIST = 128


def rel_bucket(dist):
    exact = REL_BUCKETS // 2
    d = jnp.maximum(dist, 0)
    logd = jnp.log(jnp.maximum(d, 1).astype(F32) / exact)
    far = exact + (logd / math.log(REL_MAX_DIST / exact) * (REL_BUCKETS - exact)).astype(jnp.int32)
    return jnp.where(d < exact, d, jnp.minimum(far, REL_BUCKETS - 1))


def head_bias(table, dist):
    onehot = (rel_bucket(dist)[..., None] == jnp.arange(REL_BUCKETS)).astype(F32)
    out = jnp.dot(onehot, table.astype(F32), precision=HIGHEST)
    return jnp.moveaxis(out, -1, 0)


def local_bias(table, n_prev, window, groups):
    width = (n_prev + 1) * ATT_BLOCK
    dist = n_prev * ATT_BLOCK + jnp.arange(ATT_BLOCK)[:, None] - jnp.arange(width)[None, :]
    bias = jnp.where((dist >= 0) & (dist <= window), head_bias(table, dist), NEG)
    return bias.reshape(groups, (ATT_HEADS // groups) * ATT_BLOCK, width)


def _banded_kernel(q_ref, k_ref, v_ref, bias_ref, sink_ref, o_ref, *, groups, n_prev, use_sink):
    qi = pl.program_id(1)
    rep = ATT_HEADS // groups
    scale = HEAD_DIM ** -0.5
    for g in range(groups):
        qs = jnp.concatenate(
            [q_ref[:, (g * rep + r) * HEAD_DIM:(g * rep + r + 1) * HEAD_DIM] for r in range(rep)], axis=0)
        qs = (qs * scale).astype(BF16)
        logits = []
        vals = []
        for s in range(n_prev + 1):
            kb = qi - n_prev + s
            start = pl.multiple_of(jnp.maximum(kb, 0) * ATT_BLOCK, ATT_BLOCK)
            kblk = k_ref[pl.ds(start, ATT_BLOCK), g * HEAD_DIM:(g + 1) * HEAD_DIM].astype(BF16)
            vals.append(v_ref[pl.ds(start, ATT_BLOCK), g * HEAD_DIM:(g + 1) * HEAD_DIM].astype(BF16))
            l_s = _dot_nt(qs, kblk) + bias_ref[g, :, s * ATT_BLOCK:(s + 1) * ATT_BLOCK]
            logits.append(jnp.where(kb >= 0, l_s, NEG))
        m = functools.reduce(jnp.maximum, [jnp.max(l, axis=-1, keepdims=True) for l in logits])
        if use_sink:
            sink = sink_ref[g][:, 0:1]
            m = jnp.maximum(m, sink)
            denom = jnp.exp(sink - m)
        else:
            denom = jnp.zeros_like(m)
        acc = jnp.zeros((rep * ATT_BLOCK, HEAD_DIM), F32)
        for l_s, vblk in zip(logits, vals):
            p = jnp.exp(l_s - m)
            denom = denom + jnp.sum(p, axis=-1, keepdims=True)
            acc = acc + _dot(p.astype(BF16), vblk)
        out = acc / denom
        for r in range(rep):
            h = g * rep + r
            o_ref[:, h * HEAD_DIM:(h + 1) * HEAD_DIM] = out[r * ATT_BLOCK:(r + 1) * ATT_BLOCK, :].astype(o_ref.dtype)


def banded_attention(proj, batch, seq, q_blk, k_blk, v_blk, groups, n_prev, bias, sink_rows, out_dtype):
    nq = seq // ATT_BLOCK
    kvw = groups * HEAD_DIM
    rep = ATT_HEADS // groups
    use_sink = sink_rows is not None
    if sink_rows is None:
        sink_rows = jnp.zeros((groups, rep * ATT_BLOCK, LANE), F32)
    return pl.pallas_call(
        functools.partial(_banded_kernel, groups=groups, n_prev=n_prev, use_sink=use_sink),
        grid=(batch, nq),
        in_specs=[
            pl.BlockSpec((ATT_BLOCK, ATT_HEADS * HEAD_DIM), lambda b, i: (b * nq + i, q_blk)),
            pl.BlockSpec((seq, kvw), lambda b, i: (b, k_blk)),
            pl.BlockSpec((seq, kvw), lambda b, i: (b, v_blk)),
            pl.BlockSpec(bias.shape, lambda b, i: (0, 0, 0)),
            pl.BlockSpec(sink_rows.shape, lambda b, i: (0, 0, 0)),
        ],
        out_specs=pl.BlockSpec((ATT_BLOCK, ATT_HEADS * HEAD_DIM), lambda b, i: (b * nq + i, 0)),
        out_shape=jax.ShapeDtypeStruct((batch * seq, ATT_HEADS * HEAD_DIM), out_dtype),
        compiler_params=_cparams("parallel", "arbitrary"),
        name="banded_attention",
    )(proj, proj, proj, bias, sink_rows)


def _decode_kernel(q_ref, kn_ref, vn_ref, kb_ref, vb_ref, bias_ref, bnew_ref, sink_ref, o_ref, *, groups, use_sink):
    rep = ATT_HEADS // groups
    scale = HEAD_DIM ** -0.5
    kall = kb_ref[0].astype(BF16)
    vall = vb_ref[0].astype(BF16)
    row_head = lax.rem(lax.broadcasted_iota(jnp.int32, (1, kall.shape[0]), 1), groups)
    for g in range(groups):
        qg = jnp.concatenate(
            [q_ref[0, :, (g * rep + r) * HEAD_DIM:(g * rep + r + 1) * HEAD_DIM] for r in range(rep)], axis=0)
        qg = qg * scale
        knew = kn_ref[0, :, g * HEAD_DIM:(g + 1) * HEAD_DIM]
        vnew = vn_ref[0, :, g * HEAD_DIM:(g + 1) * HEAD_DIM]
        l_c = jnp.where(row_head == g, _dot_nt(qg.astype(BF16), kall) + bias_ref[g * rep:(g + 1) * rep, :], NEG)
        l_n = jnp.sum(qg * knew, axis=-1, keepdims=True) + bnew_ref[g * rep:(g + 1) * rep, 0:1]
        m = jnp.maximum(jnp.max(l_c, axis=-1, keepdims=True), l_n)
        if use_sink:
            sink = sink_ref[g * rep:(g + 1) * rep, 0:1]
            m = jnp.maximum(m, sink)
            denom = jnp.exp(sink - m)
        else:
            denom = jnp.zeros_like(m)
        p_c = jnp.exp(l_c - m)
        p_n = jnp.exp(l_n - m)
        denom = denom + jnp.sum(p_c, axis=-1, keepdims=True) + p_n
        out = (_dot(p_c.astype(BF16), vall) + p_n * vnew) / denom
        for r in range(rep):
            h = g * rep + r
            o_ref[0, :, h * HEAD_DIM:(h + 1) * HEAD_DIM] = out[r:r + 1, :].astype(o_ref.dtype)


def decode_attention(proj, q_blk, k_blk, v_blk, k_buf, v_buf, groups, table, sink, out_dtype):
    nb, wn = k_buf.shape[0], k_buf.shape[1]
    kvw = groups * HEAD_DIM
    bias = jnp.repeat(head_bias(table, wn - jnp.arange(wn)), groups, axis=1)
    bnew = jnp.broadcast_to(table[0][:, None].astype(F32), (ATT_HEADS, LANE))
    use_sink = sink is not None
    sink_rows = jnp.broadcast_to((sink if use_sink else jnp.zeros((ATT_HEADS,), F32))[:, None].astype(F32), (ATT_HEADS, LANE))
    full = lambda shape: pl.BlockSpec(shape, lambda b: (0,) * len(shape))
    proj3 = proj.reshape(nb, 1, -1)
    out = pl.pallas_call(
        functools.partial(_decode_kernel, groups=groups, use_sink=use_sink),
        grid=(nb,),
        in_specs=[
            pl.BlockSpec((1, 1, ATT_HEADS * HEAD_DIM), lambda b: (b, 0, q_blk)),
            pl.BlockSpec((1, 1, kvw), lambda b: (b, 0, k_blk)),
            pl.BlockSpec((1, 1, kvw), lambda b: (b, 0, v_blk)),
            pl.BlockSpec((1, wn * groups, HEAD_DIM), lambda b: (b, 0, 0)),
            pl.BlockSpec((1, wn * groups, HEAD_DIM), lambda b: (b, 0, 0)),
            full((ATT_HEADS, wn * groups)), full((ATT_HEADS, LANE)), full((ATT_HEADS, LANE)),
        ],
        out_specs=pl.BlockSpec((1, 1, ATT_HEADS * HEAD_DIM), lambda b: (b, 0, 0)),
        out_shape=jax.ShapeDtypeStruct((nb, 1, ATT_HEADS * HEAD_DIM), out_dtype),
        compiler_params=_cparams("parallel"),
        name="decode_attention",
    )(proj3, proj3, proj3, k_buf.reshape(nb, wn * groups, HEAD_DIM), v_buf.reshape(nb, wn * groups, HEAD_DIM),
      bias, bnew, sink_rows)
    return out.reshape(nb, ATT_HEADS * HEAD_DIM)


SWA_KV = 4
SWA_WINDOW = 128


def swa_layer(xp, xs, batch, seq, norm_g_mix, table, w_qkv, sink, w_o, k_buf, v_buf):
    w_qkv_b = w_qkv.astype(BF16)
    w_o_b = w_o.astype(BF16)
    kvw = SWA_KV * HEAD_DIM
    qd = ATT_HEADS * HEAD_DIM
    rep = ATT_HEADS // SWA_KV
    n_prev = -(-SWA_WINDOW // ATT_BLOCK)
    bias = local_bias(table, n_prev, SWA_WINDOW, SWA_KV)
    sink_rows = jnp.broadcast_to(jnp.repeat(sink.astype(F32), ATT_BLOCK).reshape(SWA_KV, rep * ATT_BLOCK, 1),
                                 (SWA_KV, rep * ATT_BLOCK, LANE))
    proj_p = norm_proj(xp, norm_g_mix, w_qkv_b)
    o_p = banded_attention(proj_p, batch, seq, 0, qd // kvw, qd // kvw + 1, SWA_KV, n_prev, bias, sink_rows, BF16)
    xp = proj_res(o_p, w_o_b, xp)
    buf = min(SWA_WINDOW, seq)
    kv_p = proj_p.reshape(batch, seq, -1)[:, seq - buf:, qd:]
    k_p = kv_p[..., :kvw].reshape(batch, buf, SWA_KV, HEAD_DIM)
    v_p = kv_p[..., kvw:2 * kvw].reshape(batch, buf, SWA_KV, HEAD_DIM)
    proj_s = norm_proj(xs, norm_g_mix, w_qkv_b)
    o_s = decode_attention(proj_s, 0, qd // kvw, qd // kvw + 1, k_buf, v_buf, SWA_KV, table, sink, BF16)
    xs = proj_res(o_s, w_o_b, xs)
    nb = xs.shape[0]
    k_s = jnp.concatenate([k_buf[:, 1:], proj_s[:, None, qd:qd + kvw].reshape(nb, 1, SWA_KV, HEAD_DIM)], axis=1)
    v_s = jnp.concatenate([v_buf[:, 1:], proj_s[:, None, qd + kvw:qd + 2 * kvw].reshape(nb, 1, SWA_KV, HEAD_DIM)], axis=1)
    return xp, xs, k_p, v_p, k_s, v_s


GDN_K_HEADS = 16
GDN_V_HEADS = 32
GDN_D = 128
GDN_KEY = GDN_K_HEADS * GDN_D
GDN_VAL = GDN_V_HEADS * GDN_D
GDN_CONV_DIM = 2 * GDN_KEY + GDN_VAL
GDN_CHUNK = 64
GDN_BETA_BLK = (GDN_CONV_DIM + GDN_VAL) // LANE
GDN_GROUP = 8


def gdn_weights(w_in):
    c2 = GDN_CONV_DIM + GDN_VAL
    parts = [w_in[:, :c2], _pad_cols(w_in[:, c2:c2 + GDN_V_HEADS], LANE), _pad_cols(w_in[:, c2 + GDN_V_HEADS:], LANE)]
    w = jnp.concatenate(parts, axis=1)
    return _pad_cols(w, _round_up(w.shape[1], COL_TILE)).astype(BF16)


def _split_bf16(a):
    hi = a.astype(BF16)
    return hi, (a - hi.astype(F32)).astype(BF16)


def _dot3(a, b):
    ah, al = _split_bf16(a)
    bh, bl = _split_bf16(b)
    return _dot(ah, bh) + (_dot(ah, bl) + _dot(al, bh))


def _unit_lower_inverses(ms):
    n = ms[0].shape[0]
    eye = jnp.where(_eye(n), 1.0, 0.0)
    ps = [-m for m in ms]
    ts = [eye + p for p in ps]
    ps = [_dot3(p, p) for p in ps]
    k = 2
    while k < n:
        if 2 * k >= n:
            ts = [t + _dot3(t, p) for t, p in zip(ts, ps)]
        else:
            both = [_dot3(jnp.concatenate([t, p], axis=0), p) for t, p in zip(ts, ps)]
            ts = [t + b[:n] for t, b in zip(ts, both)]
            ps = [b[n:] for b in both]
        k *= 2
    return ts


def _l2n(x):
    return x * lax.rsqrt(jnp.sum(x * x, axis=-1, keepdims=True) + EPS)


def _gdn_prompt_kernel(qk_ref, v_ref, z_ref, beta_ref, a_ref, cw_ref, dtb_ref, alog_ref, ng_ref,
                       o_ref, st_ref, xpad_ref, s_ref):
    c = pl.program_id(1)
    L = GDN_CHUNK

    @pl.when(c == 0)
    def _():
        xpad_ref[0:8, :] = jnp.zeros((8, GDN_CONV_DIM), F32)
        s_ref[...] = jnp.zeros_like(s_ref)

    xpad_ref[8:8 + L, 0:2 * GDN_KEY] = qk_ref[...]
    xpad_ref[8:8 + L, 2 * GDN_KEY:GDN_CONV_DIM] = v_ref[...]
    conv = cw_ref[3:4, :] * xpad_ref[8:8 + L, :]
    for j in range(3):
        conv = conv + cw_ref[j:j + 1, :] * xpad_ref[pl.ds(5 + j, L), :]
    xpad_ref[0:8, :] = xpad_ref[L:L + 8, :]
    act = _silu(conv)

    beta = _sigmoid(beta_ref[...])
    g = -jnp.exp(alog_ref[...]) * _softplus(a_ref[...] + dtb_ref[...])
    row = lax.broadcasted_iota(jnp.int32, (L, L), 0)
    col = lax.broadcasted_iota(jnp.int32, (L, L), 1)
    incl = row >= col
    strict = row > col
    gc = _dot(jnp.where(incl, 1.0, 0.0).astype(F32), g, precision=HIGHEST)
    gc_t = gc.T
    eg = jnp.exp(gc)
    g_last = gc[L - 1:L, :]
    k_scale = jnp.exp(g_last - gc)
    s_scale = jnp.exp(g_last)
    ng = ng_ref[...]

    rep = GDN_V_HEADS // GDN_K_HEADS
    for grp in range(GDN_V_HEADS // GDN_GROUP):
        vhs = list(range(grp * GDN_GROUP, (grp + 1) * GDN_GROUP))
        khs = list(range(vhs[0] // rep, vhs[-1] // rep + 1))
        qn = {kh: _l2n(act[:, kh * GDN_D:(kh + 1) * GDN_D]) * GDN_D ** -0.5 for kh in khs}
        kn = {kh: _l2n(act[:, GDN_KEY + kh * GDN_D:GDN_KEY + (kh + 1) * GDN_D]) for kh in khs}
        kn_b = {kh: kn[kh].astype(BF16) for kh in khs}
        prods = {kh: _dot_nt(jnp.concatenate([kn_b[kh], qn[kh].astype(BF16)], axis=0), kn_b[kh]) for kh in khs}
        bcol = {vh: beta[:, vh:vh + 1] for vh in vhs}
        egcol = {vh: eg[:, vh:vh + 1] for vh in vhs}
        decay = {vh: jnp.exp(jnp.where(incl, gc[:, vh:vh + 1] - gc_t[vh:vh + 1, :], NEG)) for vh in vhs}
        t_inv = _unit_lower_inverses([jnp.where(strict, prods[vh // rep][:L] * bcol[vh] * decay[vh], 0.0) for vh in vhs])
        rhs = [jnp.concatenate([act[:, 2 * GDN_KEY + vh * GDN_D:2 * GDN_KEY + (vh + 1) * GDN_D] * bcol[vh],
                                kn[vh // rep] * (bcol[vh] * egcol[vh])], axis=1).astype(BF16) for vh in vhs]
        uw = [_dot(t.astype(BF16), r) for t, r in zip(t_inv, rhs)]
        s_old = [s_ref[vh] for vh in vhs]
        ws_qs = [_dot(jnp.concatenate([uw_h[:, GDN_D:], qn[vh // rep] * egcol[vh]], axis=0).astype(BF16), s.astype(BF16))
                 for vh, uw_h, s in zip(vhs, uw, s_old)]
        v_new = [(uw_h[:, :GDN_D] - wq[:L]).astype(BF16) for uw_h, wq in zip(uw, ws_qs)]
        outs = [wq[L:] + _dot((prods[vh // rep][L:] * decay[vh]).astype(BF16), vn)
                for vh, wq, vn in zip(vhs, ws_qs, v_new)]
        upd = [_dot_tn((kn[vh // rep] * k_scale[:, vh:vh + 1]).astype(BF16), vn) for vh, vn in zip(vhs, v_new)]
        for vh, s, u, o in zip(vhs, s_old, upd, outs):
            s_ref[vh] = s * s_scale[:, vh:vh + 1] + u
            zz = z_ref[:, vh * GDN_D:(vh + 1) * GDN_D]
            o_ref[:, vh * GDN_D:(vh + 1) * GDN_D] = (_rms(o, ng) * _silu(zz)).astype(BF16)

    @pl.when(c == pl.num_programs(1) - 1)
    def _():
        st_ref[0] = s_ref[...]


def gdn_prompt(proj, batch, seq, conv_w, dt_bias, a_log, norm_g):
    L = GDN_CHUNK
    nc = seq // L
    rowb = lambda b, c: b * nc + c
    full = lambda shape: pl.BlockSpec(shape, lambda b, c: (0,) * len(shape))
    return pl.pallas_call(
        _gdn_prompt_kernel,
        grid=(batch, nc),
        in_specs=[
            pl.BlockSpec((L, 2 * GDN_KEY), lambda b, c: (rowb(b, c), 0)),
            pl.BlockSpec((L, GDN_VAL), lambda b, c: (rowb(b, c), 1)),
            pl.BlockSpec((L, GDN_VAL), lambda b, c: (rowb(b, c), 2)),
            pl.BlockSpec((L, LANE), lambda b, c: (rowb(b, c), GDN_BETA_BLK)),
            pl.BlockSpec((L, LANE), lambda b, c: (rowb(b, c), GDN_BETA_BLK + 1)),
            full((4, GDN_CONV_DIM)), full((1, LANE)), full((1, LANE)), full((1, GDN_D)),
        ],
        out_specs=[
            pl.BlockSpec((L, GDN_VAL), lambda b, c: (rowb(b, c), 0)),
            pl.BlockSpec((1, GDN_V_HEADS, GDN_D, GDN_D), lambda b, c: (b, 0, 0, 0)),
        ],
        out_shape=[
            jax.ShapeDtypeStruct((batch * seq, GDN_VAL), BF16),
            jax.ShapeDtypeStruct((batch, GDN_V_HEADS, GDN_D, GDN_D), F32),
        ],
        scratch_shapes=[pltpu.VMEM((L + 8, GDN_CONV_DIM), F32), pltpu.VMEM((GDN_V_HEADS, GDN_D, GDN_D), F32)],
        compiler_params=_cparams("parallel", "arbitrary"),
        name="gdn_prompt",
    )(proj, proj, proj, proj, proj, conv_w, _pad_lanes(dt_bias), _pad_lanes(a_log), norm_g.reshape(1, -1))


def _gdn_sample_pre_kernel(qk_ref, v_ref, beta_ref, a_ref, prev_ref, cw_ref, dtb_ref, alog_ref,
                           q_ref, k_ref, vo_ref, bo_ref, eg_ref):
    xnew = jnp.concatenate([qk_ref[...], v_ref[...]], axis=1)
    conv = cw_ref[3:4, :] * xnew
    for j in range(3):
        conv = conv + cw_ref[j:j + 1, :] * prev_ref[j]
    act = _silu(conv)
    for kh in range(GDN_K_HEADS):
        q_ref[:, kh * GDN_D:(kh + 1) * GDN_D] = _l2n(act[:, kh * GDN_D:(kh + 1) * GDN_D]) * GDN_D ** -0.5
        k_ref[:, kh * GDN_D:(kh + 1) * GDN_D] = _l2n(act[:, GDN_KEY + kh * GDN_D:GDN_KEY + (kh + 1) * GDN_D])
    vo_ref[...] = act[:, 2 * GDN_KEY:]
    bo_ref[...] = _sigmoid(beta_ref[...])
    eg_ref[...] = jnp.exp(-jnp.exp(alog_ref[...]) * _softplus(a_ref[...] + dtb_ref[...]))


def _gdn_sample_state_kernel(s_ref, q_ref, k_ref, v_ref, z_ref, beta_ref, eg_ref, ng_ref, so_ref, o_ref):
    eye = _eye(GDN_D)
    ng = ng_ref[...]
    for kh in range(GDN_K_HEADS):
        qrow = q_ref[0, :, kh * GDN_D:(kh + 1) * GDN_D]
        krow = k_ref[0, :, kh * GDN_D:(kh + 1) * GDN_D]
        qcol = _row_to_col(qrow, eye)
        kcol = _row_to_col(krow, eye)
        qk = jnp.sum(qrow * krow, axis=-1, keepdims=True)
        for vh in (2 * kh, 2 * kh + 1):
            s_old = s_ref[0, vh]
            beta = beta_ref[0, :, vh:vh + 1]
            eg = eg_ref[0, :, vh:vh + 1]
            v = v_ref[0, :, vh * GDN_D:(vh + 1) * GDN_D]
            ks = jnp.sum(kcol * s_old, axis=0, keepdims=True)
            qs = jnp.sum(qcol * s_old, axis=0, keepdims=True)
            v_new = beta * v - (beta * eg) * ks
            o = eg * qs + qk * v_new
            so_ref[0, vh] = s_old * eg + kcol * v_new
            zz = z_ref[0, :, vh * GDN_D:(vh + 1) * GDN_D]
            o_ref[0, :, vh * GDN_D:(vh + 1) * GDN_D] = (_rms(o, ng) * _silu(zz)).astype(BF16)


def gdn_sample(proj, conv_prev, state, conv_w, dt_bias, a_log, norm_g):
    nb = proj.shape[0]
    full = lambda shape: pl.BlockSpec(shape, lambda *_: (0,) * len(shape))
    row_spec = lambda width, blk: pl.BlockSpec((nb, width), lambda *_: (0, blk))
    sds = lambda width: jax.ShapeDtypeStruct((nb, width), F32)
    qn, kn, v, beta, eg = pl.pallas_call(
        _gdn_sample_pre_kernel,
        grid=(1,),
        in_specs=[
            row_spec(2 * GDN_KEY, 0), row_spec(GDN_VAL, 1), row_spec(LANE, GDN_BETA_BLK), row_spec(LANE, GDN_BETA_BLK + 1),
            full((3, nb, GDN_CONV_DIM)), full((4, GDN_CONV_DIM)), full((1, LANE)), full((1, LANE)),
        ],
        out_specs=[full((nb, GDN_KEY)), full((nb, GDN_KEY)), full((nb, GDN_VAL)), full((nb, LANE)), full((nb, LANE))],
        out_shape=[sds(GDN_KEY), sds(GDN_KEY), sds(GDN_VAL), sds(LANE), sds(LANE)],
        compiler_params=_cparams("arbitrary"),
        name="gdn_sample_pre",
    )(proj, proj, proj, proj, jnp.swapaxes(conv_prev, 0, 1), conv_w, _pad_lanes(dt_bias), _pad_lanes(a_log))
    per_row = lambda width, blk=0: pl.BlockSpec((1, 1, width), lambda b: (b, 0, blk))
    st_spec = pl.BlockSpec((1, GDN_V_HEADS, GDN_D, GDN_D), lambda b: (b, 0, 0, 0))
    r3 = lambda a: a.reshape(nb, 1, -1)
    s_new, o = pl.pallas_call(
        _gdn_sample_state_kernel,
        grid=(nb,),
        in_specs=[st_spec, per_row(GDN_KEY), per_row(GDN_KEY), per_row(GDN_VAL), per_row(GDN_VAL, 2),
                  per_row(LANE), per_row(LANE), pl.BlockSpec((1, GDN_D), lambda b: (0, 0))],
        out_specs=[st_spec, per_row(GDN_VAL)],
        out_shape=[jax.ShapeDtypeStruct(state.shape, F32), jax.ShapeDtypeStruct((nb, 1, GDN_VAL), BF16)],
        compiler_params=_cparams("parallel"),
        name="gdn_sample_state",
    )(state, r3(qn), r3(kn), r3(v), r3(proj), r3(beta), r3(eg), norm_g.reshape(1, -1))
    return o.reshape(nb, GDN_VAL), s_new


def gdn_layer(xp, xs, batch, seq, norm_g_mix, w_in, conv_w, dt_bias, a_log, norm_g, w_out, conv_state, state):
    w_in_b = gdn_weights(w_in)
    w_out_b = w_out.astype(BF16)
    args = (conv_w, dt_bias, a_log, norm_g)
    proj_p = norm_proj(xp, norm_g_mix, w_in_b)
    o_p, st_p = gdn_prompt(proj_p, batch, seq, *args)
    xp = proj_res(o_p, w_out_b, xp)
    raw_p = proj_p.reshape(batch, seq, -1)[:, seq - 3:, :GDN_CONV_DIM]
    proj_s = norm_proj(xs, norm_g_mix, w_in_b)
    o_s, st_s = gdn_sample(proj_s, conv_state, state, *args)
    xs = proj_res(o_s, w_out_b, xs)
    raw_s = jnp.concatenate([conv_state[:, 1:], proj_s[:, None, :GDN_CONV_DIM]], axis=1)
    return xp, xs, raw_p, st_p, raw_s, st_s


NSA_KV = 2
NSA_BLOCK = 64
NSA_TOPN = 16
NSA_WINDOW = 512
NSA_HID = 256
NSA_FORCE = 1e4
NSA_REP = ATT_HEADS // NSA_KV
NSA_KVW = NSA_KV * HEAD_DIM
NSA_QD = ATT_HEADS * HEAD_DIM
NSA_GATE_BLK = (NSA_QD + 6 * NSA_KVW) // LANE
PAGE_SIZE = 128
PAGES_PER_STEP = 128
NSA_KEY_SPAN = 512
SEL_PER_STEP = 4


def _compress_block_rows(x_refs, pos_ref, w1_ref, w2_ref, o_ref):
    for g, x_ref in enumerate(x_refs):
        nblk = x_ref.shape[0] // NSA_BLOCK
        acc = jnp.zeros((nblk, NSA_HID), F32)
        for t in range(NSA_BLOCK):
            lhs = x_ref[pl.ds(t, nblk, stride=NSA_BLOCK), :] + pos_ref[t:t + 1, :]
            acc = acc + _dot(lhs.astype(BF16), w1_ref[t * HEAD_DIM:(t + 1) * HEAD_DIM, :])
        o_ref[:, g * HEAD_DIM:(g + 1) * HEAD_DIM] = _dot(_silu(acc).astype(BF16), w2_ref[...])


def _compress_rows_kernel(x0_ref, x1_ref, pos_ref, w1_ref, w2_ref, o_ref):
    _compress_block_rows((x0_ref, x1_ref), pos_ref, w1_ref, w2_ref, o_ref)


def compress_rows(proj, col_blk, pos, w1, w2):
    m = proj.shape[0]
    rows = min(m, 8192)
    full = lambda shape: pl.BlockSpec(shape, lambda i: (0,) * len(shape))
    head = lambda g: pl.BlockSpec((rows, HEAD_DIM), lambda i: (i, col_blk * NSA_KV + g))
    return pl.pallas_call(
        _compress_rows_kernel,
        grid=(m // rows,),
        in_specs=[head(0), head(1), full(pos.shape), full(w1.shape), full(w2.shape)],
        out_specs=pl.BlockSpec((rows // NSA_BLOCK, NSA_KVW), lambda i: (i, 0)),
        out_shape=jax.ShapeDtypeStruct((m // NSA_BLOCK, NSA_KVW), F32),
        compiler_params=_cparams("parallel"),
        name="compress_rows",
    )(proj, proj, pos, w1.astype(BF16), w2.astype(BF16))


SUB_BLOCKS = PAGE_SIZE // NSA_BLOCK
BLOCK_ROWS = NSA_BLOCK * NSA_KV


def _compress_paged_kernel(start_ref, pool_ref, pos_ref, w1_ref, w2_ref, o_ref, buf_ref, sem, *, pages):
    i = pl.program_id(0)
    steps = pl.num_programs(0)
    nblk = pages * SUB_BLOCKS

    def block_copy(step, slot, k):
        start = pl.multiple_of(start_ref[step * nblk + k], BLOCK_ROWS)
        return pltpu.make_async_copy(pool_ref.at[pl.ds(start, BLOCK_ROWS), :], buf_ref.at[slot, :, k, :], sem.at[slot])

    def start_step(step, slot):
        def body(k, carry):
            block_copy(step, slot, k).start()
            return carry
        lax.fori_loop(0, nblk, body, 0)

    def wait_step(step, slot):
        def body(k, carry):
            block_copy(step, slot, k).wait()
            return carry
        lax.fori_loop(0, nblk, body, 0)

    slot = lax.rem(i, 2)

    @pl.when(i == 0)
    def _():
        start_step(0, 0)

    @pl.when(i + 1 < steps)
    def _():
        start_step(i + 1, 1 - slot)

    wait_step(i, slot)
    for g in range(NSA_KV):
        acc = jnp.zeros((nblk, NSA_HID), F32)
        for t in range(0, NSA_BLOCK, 2):
            lhs = jnp.concatenate([(buf_ref[slot, NSA_KV * (t + u) + g] + pos_ref[t + u:t + u + 1, :]).astype(BF16)
                                   for u in range(2)], axis=1)
            acc = acc + _dot(lhs, w1_ref[t * HEAD_DIM:(t + 2) * HEAD_DIM, :])
        o_ref[:, g * HEAD_DIM:(g + 1) * HEAD_DIM] = _dot(_silu(acc).astype(BF16), w2_ref[...])


def compress_paged(pool, page_table, pos, w1, w2):
    pt = page_table.reshape(-1)
    pages = min(PAGES_PER_STEP, pt.shape[0])
    steps = pt.shape[0] // pages
    nblk = pages * SUB_BLOCKS
    starts = ((pt[:, None] * SUB_BLOCKS + jnp.arange(SUB_BLOCKS, dtype=pt.dtype)[None, :]) * BLOCK_ROWS).reshape(-1)
    full = lambda shape: pl.BlockSpec(shape, lambda i, pt: (0,) * len(shape))
    return pl.pallas_call(
        functools.partial(_compress_paged_kernel, pages=pages),
        grid_spec=pltpu.PrefetchScalarGridSpec(
            num_scalar_prefetch=1,
            grid=(steps,),
            in_specs=[pl.BlockSpec(memory_space=pl.ANY), full(pos.shape), full(w1.shape), full(w2.shape)],
            out_specs=pl.BlockSpec((nblk, NSA_KVW), lambda i, pt: (i, 0)),
            scratch_shapes=[pltpu.VMEM((2, BLOCK_ROWS, nblk, HEAD_DIM), F32), pltpu.SemaphoreType.DMA((2,))],
        ),
        out_shape=jax.ShapeDtypeStruct((steps * nblk, NSA_KVW), F32),
        compiler_params=_cparams("arbitrary"),
        name="compress_paged",
    )(starts, pool.reshape(-1, HEAD_DIM), pos, w1.astype(BF16), w2.astype(BF16))


def _top_n(score, lane, n):
    sel = jnp.zeros(score.shape, F32)
    for _ in range(n):
        m = jnp.max(score, axis=-1, keepdims=True)
        j = jnp.min(jnp.where(score == m, lane, 1 << 30), axis=-1, keepdims=True)
        pick = lane == j
        sel = jnp.where(pick, 1.0, sel)
        score = jnp.where(pick, -jnp.inf, score)
    return sel


def _nsa_prompt_kernel(q_ref, win_ref, gate_ref, kc_ref, vc_ref, ks_ref, vs_ref, bc_ref, bl_ref, far_ref, o_ref):
    qi = pl.program_id(1)
    seq = ks_ref.shape[0]
    n_cmp = kc_ref.shape[0]
    scale = HEAD_DIM ** -0.5
    gates = _sigmoid(gate_ref[...])
    qpos = qi * ATT_BLOCK + lax.broadcasted_iota(jnp.int32, (ATT_BLOCK, n_cmp), 0)
    blk = lax.broadcasted_iota(jnp.int32, (ATT_BLOCK, n_cmp), 1)
    cur = lax.shift_right_logical(qpos, 6)
    valid = blk <= cur
    forced = valid & ((blk == 0) | (blk >= cur - 1))
    qh_all, o_cmp_all, sel_all = [], [], []
    for g in range(NSA_KV):
        kcg = kc_ref[:, g * HEAD_DIM:(g + 1) * HEAD_DIM].astype(BF16)
        vcg = vc_ref[:, g * HEAD_DIM:(g + 1) * HEAD_DIM].astype(BF16)
        qh = [(q_ref[:, (g * NSA_REP + r) * HEAD_DIM:(g * NSA_REP + r + 1) * HEAD_DIM] * scale).astype(BF16)
              for r in range(NSA_REP)]
        bias_c = bc_ref[g, 0]
        ok_c = bias_c > 0.5 * NEG
        lc = _dot_nt(jnp.concatenate(qh, axis=0), kcg) + bias_c
        p = jnp.where(ok_c, jnp.exp(lc - jnp.max(lc, axis=-1, keepdims=True)), 0.0)
        p = p / jnp.maximum(jnp.sum(p, axis=-1, keepdims=True), 1e-30)
        p_slc = functools.reduce(lambda a, b: a + b, [p[r * ATT_BLOCK:(r + 1) * ATT_BLOCK] for r in range(NSA_REP)])
        score = jnp.where(forced, NSA_FORCE, jnp.where(valid, p_slc, -1.0))
        qh_all.append(qh)
        o_cmp_all.append(_dot(p.astype(BF16), vcg))
        sel_all.append(jnp.where(valid, _top_n(score, blk, NSA_TOPN), 0.0).astype(BF16))

    def selected_and_combine(n_keys):
        er = lax.broadcasted_iota(jnp.int32, (n_cmp, n_keys), 0)
        ec = lax.broadcasted_iota(jnp.int32, (n_cmp, n_keys), 1)
        expand = jnp.where(lax.shift_right_logical(ec, 6) == er, 1.0, 0.0).astype(BF16)
        for g in range(NSA_KV):
            sel_bias = jnp.where(_dot(sel_all[g], expand) > 0.5, 0.0, NEG)
            ksg = ks_ref[0:n_keys, g * HEAD_DIM:(g + 1) * HEAD_DIM].astype(BF16)
            vsg = vs_ref[0:n_keys, g * HEAD_DIM:(g + 1) * HEAD_DIM].astype(BF16)
            for r in range(NSA_REP):
                h = g * NSA_REP + r
                pieces = []
                for c in range(n_keys // ATT_BLOCK):
                    delta = qi - c
                    far = jnp.where(delta >= 2, far_ref[h:h + 1, 0:1], NEG)
                    pieces.append(jnp.where(delta == 0, bl_ref[h, :, ATT_BLOCK:2 * ATT_BLOCK],
                                            jnp.where(delta == 1, bl_ref[h, :, 0:ATT_BLOCK], far)))
                ls = _dot_nt(qh_all[g][r], ksg) + jnp.concatenate(pieces, axis=1) + sel_bias
                ps = jnp.exp(ls - jnp.max(ls, axis=-1, keepdims=True))
                den = jnp.sum(ps, axis=-1, keepdims=True)
                o_sel = _dot(ps.astype(BF16), vsg) / den
                out = (gates[:, 3 * h:3 * h + 1] * o_cmp_all[g][r * ATT_BLOCK:(r + 1) * ATT_BLOCK]
                       + gates[:, 3 * h + 1:3 * h + 2] * o_sel
                       + gates[:, 3 * h + 2:3 * h + 3] * win_ref[:, h * HEAD_DIM:(h + 1) * HEAD_DIM])
                o_ref[:, h * HEAD_DIM:(h + 1) * HEAD_DIM] = out.astype(BF16)

    tiles_per_span = NSA_KEY_SPAN // ATT_BLOCK
    for c in range(seq // NSA_KEY_SPAN):
        pl.when(qi // tiles_per_span == c)(functools.partial(selected_and_combine, (c + 1) * NSA_KEY_SPAN))


def nsa_prompt_attention(proj, o_win, kc_blk, vc_blk, table, batch, seq):
    nq = seq // ATT_BLOCK
    n_cmp = seq // NSA_BLOCK
    assert n_cmp >= NSA_TOPN and seq % NSA_KEY_SPAN == 0
    dist = jnp.arange(seq)[:, None] - (jnp.arange(n_cmp) * NSA_BLOCK + NSA_BLOCK - 1)[None, :]
    bias_c = jnp.where(dist >= 0, head_bias(table, dist), NEG)
    bias_c = bias_c.reshape(NSA_KV, NSA_REP, nq, ATT_BLOCK, n_cmp).transpose(0, 2, 1, 3, 4)
    bias_c = bias_c.reshape(NSA_KV, nq, NSA_REP * ATT_BLOCK, n_cmp)
    bias_l = local_bias(table, 1, 1 << 30, ATT_HEADS).reshape(ATT_HEADS, ATT_BLOCK, 2 * ATT_BLOCK)
    far = jnp.broadcast_to(table[REL_BUCKETS - 1][:, None].astype(F32), (ATT_HEADS, LANE))
    full = lambda shape: pl.BlockSpec(shape, lambda b, i: (0,) * len(shape))
    return pl.pallas_call(
        _nsa_prompt_kernel,
        grid=(batch, nq),
        in_specs=[
            pl.BlockSpec((ATT_BLOCK, NSA_QD), lambda b, i: (b * nq + i, 0)),
            pl.BlockSpec((ATT_BLOCK, NSA_QD), lambda b, i: (b * nq + i, 0)),
            pl.BlockSpec((ATT_BLOCK, LANE), lambda b, i: (b * nq + i, NSA_GATE_BLK)),
            pl.BlockSpec((n_cmp, NSA_KVW), lambda b, i: (b, 0)),
            pl.BlockSpec((n_cmp, NSA_KVW), lambda b, i: (b, 0)),
            pl.BlockSpec((seq, NSA_KVW), lambda b, i: (b, NSA_QD // NSA_KVW + 2)),
            pl.BlockSpec((seq, NSA_KVW), lambda b, i: (b, NSA_QD // NSA_KVW + 3)),
            pl.BlockSpec((NSA_KV, 1, NSA_REP * ATT_BLOCK, n_cmp), lambda b, i: (0, i, 0, 0)),
            full(bias_l.shape), full(far.shape),
        ],
        out_specs=pl.BlockSpec((ATT_BLOCK, NSA_QD), lambda b, i: (b * nq + i, 0)),
        out_shape=jax.ShapeDtypeStruct((batch * seq, NSA_QD), BF16),
        compiler_params=_cparams("parallel", "arbitrary"),
        name="nsa_prompt_attention",
    )(proj, o_win, proj, kc_blk, vc_blk, proj, proj, bias_c, bias_l, far)


def _nsa_sample_cmp_kernel(q_ref, kc_ref, vc_ref, bias_ref, o_ref, idx_ref, *, cur, n_sel):
    n_cmp = kc_ref.shape[1]
    width = _round_up(n_sel, LANE)
    scale = HEAD_DIM ** -0.5
    lane = lax.broadcasted_iota(jnp.int32, (1, width), 1)
    lane_out = lax.broadcasted_iota(jnp.int32, (1, LANE), 1)
    valid = lane <= cur
    forced = valid & ((lane == 0) | (lane >= cur - 1))
    for g in range(NSA_KV):
        qg = jnp.concatenate([q_ref[0, :, (g * NSA_REP + r) * HEAD_DIM:(g * NSA_REP + r + 1) * HEAD_DIM]
                              for r in range(NSA_REP)], axis=0) * scale
        bias = bias_ref[g * NSA_REP:(g + 1) * NSA_REP, :]
        ok = bias > 0.5 * NEG
        lc = _dot_nt(qg.astype(BF16), kc_ref[0, :, g * HEAD_DIM:(g + 1) * HEAD_DIM].astype(BF16)) + bias
        p = jnp.where(ok, jnp.exp(lc - jnp.max(lc, axis=-1, keepdims=True)), 0.0)
        p = p / jnp.maximum(jnp.sum(p, axis=-1, keepdims=True), 1e-30)
        o_cmp = _dot(p.astype(BF16), vc_ref[0, :, g * HEAD_DIM:(g + 1) * HEAD_DIM].astype(BF16))
        for r in range(NSA_REP):
            h = g * NSA_REP + r
            o_ref[0, :, h * HEAD_DIM:(h + 1) * HEAD_DIM] = o_cmp[r:r + 1, :]
        p_slc = jnp.sum(p, axis=0, keepdims=True)
        score = jnp.concatenate([p_slc, jnp.zeros((1, width - n_cmp), F32)], axis=1)
        score = jnp.where(forced, NSA_FORCE, jnp.where(valid, score, -1.0))
        score = jnp.where(lane < n_sel, score, -jnp.inf)
        idx_row = jnp.zeros((1, LANE), jnp.int32)
        for it in range(NSA_TOPN):
            m = jnp.max(score, axis=-1, keepdims=True)
            j = jnp.min(jnp.where(score == m, lane, 1 << 30), axis=-1, keepdims=True)
            idx_row = jnp.where(lane_out == it, j, idx_row)
            score = jnp.where(lane == j, -jnp.inf, score)
        idx_ref[0, :, g * LANE:(g + 1) * LANE] = idx_row


def _nsa_sample_sel_kernel(idx_ref, pt_ref, q_ref, kn_ref, vn_ref, *refs, cur, n_past_blk):
    blocks = refs[:SEL_PER_STEP * NSA_KV * 3]
    o_ref, m_ref, l_ref, acc_ref = refs[SEL_PER_STEP * NSA_KV * 3:]
    b = pl.program_id(0)
    n = pl.program_id(1)
    scale = HEAD_DIM ** -0.5

    @pl.when(n == 0)
    def _():
        m_ref[...] = jnp.full(m_ref.shape, NEG, F32)
        l_ref[...] = jnp.zeros_like(l_ref)
        acc_ref[...] = jnp.zeros_like(acc_ref)

    row = lax.broadcasted_iota(jnp.int32, (BLOCK_ROWS, HEAD_DIM), 0)
    key_head = lax.rem(lax.broadcasted_iota(jnp.int32, (1, BLOCK_ROWS), 1), NSA_KV)
    for g in range(NSA_KV):
        qg = (jnp.concatenate([q_ref[0, :, (g * NSA_REP + r) * HEAD_DIM:(g * NSA_REP + r + 1) * HEAD_DIM]
                               for r in range(NSA_REP)], axis=0) * scale).astype(BF16)
        knew = jnp.where(row == g, kn_ref[0, :, g * HEAD_DIM:(g + 1) * HEAD_DIM], 0.0)
        vnew = jnp.where(row == g, vn_ref[0, :, g * HEAD_DIM:(g + 1) * HEAD_DIM], 0.0)
        rows = slice(g * NSA_REP, (g + 1) * NSA_REP)
        logits, vals, oks = [], [], []
        for j in range(SEL_PER_STEP):
            k_ref, v_ref, bias_ref = blocks[(j * NSA_KV + g) * 3:(j * NSA_KV + g) * 3 + 3]
            i = idx_ref[(b * NSA_KV + g) * NSA_TOPN + n * SEL_PER_STEP + j]
            is_new = i >= n_past_blk
            kblk = jnp.where(is_new, knew, k_ref[...]).astype(BF16)
            vals.append(jnp.where(is_new, vnew, v_ref[...]).astype(BF16))
            bias = bias_ref[0, rows, :]
            ok = (bias > 0.5 * NEG) & (key_head == g) & (i <= cur)
            oks.append(ok)
            logits.append(jnp.where(ok, _dot_nt(qg, kblk) + bias, NEG))
        m_old = m_ref[rows, :]
        m_new = functools.reduce(jnp.maximum, [m_old] + [jnp.max(ls, axis=-1, keepdims=True) for ls in logits])
        alpha = jnp.exp(m_old - m_new)
        l_new = alpha * l_ref[rows, :]
        acc = alpha * acc_ref[rows, :]
        for ls, ok, vblk in zip(logits, oks, vals):
            ps = jnp.where(ok, jnp.exp(ls - m_new[:, 0:1]), 0.0)
            l_new = l_new + jnp.sum(ps, axis=-1, keepdims=True)
            acc = acc + _dot(ps.astype(BF16), vblk)
        l_ref[rows, :] = l_new
        acc_ref[rows, :] = acc
        m_ref[rows, :] = m_new

    @pl.when(n == pl.num_programs(1) - 1)
    def _():
        out = acc_ref[...] / jnp.maximum(l_ref[...], 1e-30)
        for h in range(ATT_HEADS):
            o_ref[0, :, h * HEAD_DIM:(h + 1) * HEAD_DIM] = out[h:h + 1, :]


def _nsa_combine_kernel(gate_ref, c_ref, s_ref, w_ref, o_ref):
    gates = _sigmoid(gate_ref[...])
    for h in range(ATT_HEADS):
        cols = slice(h * HEAD_DIM, (h + 1) * HEAD_DIM)
        out = (gates[:, 3 * h:3 * h + 1] * c_ref[:, cols] + gates[:, 3 * h + 1:3 * h + 2] * s_ref[:, cols]
               + gates[:, 3 * h + 2:3 * h + 3] * w_ref[:, cols])
        o_ref[:, cols] = out.astype(BF16)


def nsa_sample_attention(proj, kc_blk, vc_blk, sel_k_pool, sel_v_pool, win_k, win_v, page_table, table, past):
    nb = proj.shape[0]
    n_pages = page_table.shape[1]
    n_cmp = kc_blk.shape[0] // nb
    sub = PAGE_SIZE // NSA_BLOCK
    n_past_blk = n_pages * sub
    cur = past // NSA_BLOCK
    n_sel = -(-(past + 1) // NSA_BLOCK)
    proj3 = proj.reshape(nb, 1, -1)
    full = lambda shape: pl.BlockSpec(shape, lambda *_: (0,) * len(shape))
    dist_c = past - (jnp.arange(n_cmp) * NSA_BLOCK + NSA_BLOCK - 1)
    bias_c = jnp.where(dist_c >= 0, head_bias(table, dist_c), NEG)
    o_cmp, idx = pl.pallas_call(
        functools.partial(_nsa_sample_cmp_kernel, cur=cur, n_sel=n_sel),
        grid=(nb,),
        in_specs=[
            pl.BlockSpec((1, 1, NSA_QD), lambda b: (b, 0, 0)),
            pl.BlockSpec((1, n_cmp, NSA_KVW), lambda b: (b, 0, 0)),
            pl.BlockSpec((1, n_cmp, NSA_KVW), lambda b: (b, 0, 0)),
            full(bias_c.shape),
        ],
        out_specs=[pl.BlockSpec((1, 1, NSA_QD), lambda b: (b, 0, 0)), pl.BlockSpec((1, 1, NSA_KV * LANE), lambda b: (b, 0, 0))],
        out_shape=[jax.ShapeDtypeStruct((nb, 1, NSA_QD), F32), jax.ShapeDtypeStruct((nb, 1, NSA_KV * LANE), jnp.int32)],
        compiler_params=_cparams("parallel"),
        name="nsa_sample_cmp",
    )(proj3, kc_blk.reshape(nb, n_cmp, NSA_KVW), vc_blk.reshape(nb, n_cmp, NSA_KVW), bias_c)
    idx_flat = idx.reshape(nb, NSA_KV, LANE)[:, :, :NSA_TOPN].reshape(-1)

    dist_s = past - jnp.arange(n_sel * NSA_BLOCK)
    bias_s = jnp.where(dist_s >= 0, head_bias(table, dist_s), NEG).reshape(ATT_HEADS, n_sel, NSA_BLOCK)
    bias_s = jnp.repeat(jnp.swapaxes(bias_s, 0, 1), NSA_KV, axis=2)

    def chosen(b, n, j, g, idx_ref):
        return idx_ref[(b * NSA_KV + g) * NSA_TOPN + n * SEL_PER_STEP + j]

    def phys(b, n, j, g, idx_ref, pt_ref):
        i = jnp.minimum(chosen(b, n, j, g, idx_ref), n_past_blk - 1)
        return pt_ref[b, i // sub] * sub + i % sub

    pool_spec = lambda j, g: pl.BlockSpec((BLOCK_ROWS, HEAD_DIM),
                                          lambda b, n, idx_ref, pt_ref: (phys(b, n, j, g, idx_ref, pt_ref), 0))
    bias_spec = lambda j, g: pl.BlockSpec((1, ATT_HEADS, BLOCK_ROWS),
                                          lambda b, n, idx_ref, pt_ref: (chosen(b, n, j, g, idx_ref), 0, 0))
    row_spec = lambda width, blk: pl.BlockSpec((1, 1, width), lambda b, n, idx_ref, pt_ref: (b, 0, blk))
    pool_k = sel_k_pool.reshape(-1, HEAD_DIM)
    pool_v = sel_v_pool.reshape(-1, HEAD_DIM)
    block_specs, block_args = [], []
    for j in range(SEL_PER_STEP):
        for g in range(NSA_KV):
            block_specs += [pool_spec(j, g), pool_spec(j, g), bias_spec(j, g)]
            block_args += [pool_k, pool_v, bias_s]
    o_sel = pl.pallas_call(
        functools.partial(_nsa_sample_sel_kernel, cur=cur, n_past_blk=n_past_blk),
        grid_spec=pltpu.PrefetchScalarGridSpec(
            num_scalar_prefetch=2,
            grid=(nb, NSA_TOPN // SEL_PER_STEP),
            in_specs=[row_spec(NSA_QD, 0), row_spec(NSA_KVW, NSA_QD // NSA_KVW + 2),
                      row_spec(NSA_KVW, NSA_QD // NSA_KVW + 3)] + block_specs,
            out_specs=row_spec(NSA_QD, 0),
            scratch_shapes=[pltpu.VMEM((ATT_HEADS, LANE), F32), pltpu.VMEM((ATT_HEADS, LANE), F32),
                            pltpu.VMEM((ATT_HEADS, HEAD_DIM), F32)],
        ),
        out_shape=jax.ShapeDtypeStruct((nb, 1, NSA_QD), F32),
        compiler_params=_cparams("parallel", "arbitrary"),
        name="nsa_sample_sel",
    )(idx_flat, page_table, proj3, proj3, proj3, *block_args)

    o_win = decode_attention(proj, 0, NSA_QD // NSA_KVW + 4, NSA_QD // NSA_KVW + 5, win_k, win_v, NSA_KV, table, None, F32)
    row2 = lambda width, blk: pl.BlockSpec((nb, width), lambda i: (0, blk))
    return pl.pallas_call(
        _nsa_combine_kernel,
        grid=(1,),
        in_specs=[row2(LANE, NSA_GATE_BLK), row2(NSA_QD, 0), row2(NSA_QD, 0), row2(NSA_QD, 0)],
        out_specs=row2(NSA_QD, 0),
        out_shape=jax.ShapeDtypeStruct((nb, NSA_QD), BF16),
        compiler_params=_cparams("arbitrary"),
        name="nsa_combine",
    )(proj, o_cmp.reshape(nb, NSA_QD), o_sel.reshape(nb, NSA_QD), o_win)


def nsa_weights(w_in):
    return _pad_cols(w_in, _round_up(w_in.shape[1], COL_TILE)).astype(BF16)


def nsa_layer(xp, xs, batch, seq, past, norm_g_mix, table, w_in, pos_k, pos_v, w1_k, w2_k, w1_v, w2_v, w_o,
              cmp_k_pool, cmp_v_pool, sel_k_pool, sel_v_pool, win_k, win_v, page_table):
    w_in_b = nsa_weights(w_in)
    w_o_b = w_o.astype(BF16)
    kv0 = NSA_QD // NSA_KVW
    nb = xs.shape[0]

    proj_p = norm_proj(xp, norm_g_mix, w_in_b)
    kc_blk = compress_rows(proj_p, kv0, pos_k, w1_k, w2_k)
    vc_blk = compress_rows(proj_p, kv0 + 1, pos_v, w1_v, w2_v)
    n_prev = -(-NSA_WINDOW // ATT_BLOCK)
    bias_w = local_bias(table, n_prev, NSA_WINDOW, NSA_KV)
    o_win = banded_attention(proj_p, batch, seq, 0, kv0 + 4, kv0 + 5, NSA_KV, n_prev, bias_w, None, F32)
    o_p = nsa_prompt_attention(proj_p, o_win, kc_blk, vc_blk, table, batch, seq)
    xp = proj_res(o_p, w_o_b, xp)
    kvs_p = proj_p.reshape(batch, seq, -1)[:, :, NSA_QD:NSA_QD + 6 * NSA_KVW].reshape(batch, seq, 6, NSA_KV, HEAD_DIM)
    buf = min(NSA_WINDOW, seq)
    outs_p = [kvs_p[:, :, j] for j in range(4)] + [kvs_p[:, seq - buf:, 4], kvs_p[:, seq - buf:, 5]]

    proj_s = norm_proj(xs, norm_g_mix, w_in_b)
    kc_s = compress_paged(cmp_k_pool, page_table, pos_k, w1_k, w2_k)
    vc_s = compress_paged(cmp_v_pool, page_table, pos_v, w1_v, w2_v)
    o_s = nsa_sample_attention(proj_s, kc_s, vc_s, sel_k_pool, sel_v_pool, win_k, win_v, page_table, table, past)
    xs = proj_res(o_s, w_o_b, xs)
    kvs_s = proj_s[:, NSA_QD:NSA_QD + 6 * NSA_KVW].reshape(nb, 1, 6, NSA_KV, HEAD_DIM)
    outs_s = [kvs_s[:, :, j] for j in range(4)]
    outs_s += [jnp.concatenate([win_k[:, 1:], kvs_s[:, :, 4]], axis=1), jnp.concatenate([win_v[:, 1:], kvs_s[:, :, 5]], axis=1)]
    return xp, xs, outs_p, outs_s


def kernel(x_prompt, x_sample, state_ssd_conv, state_ssd, cache_swa_k, cache_swa_v, state_gdn_conv, state_gdn,
           cache_nsa_cmp_k, cache_nsa_cmp_v, cache_nsa_sel_k, cache_nsa_sel_v, cache_nsa_win_k, cache_nsa_win_v,
           page_table, rel_table, norm_ffn1, norm_mix, norm_ffn2, norm_final,
           ffn1_gate, ffn1_up, ffn1_down, ffn2_gate, ffn2_up, ffn2_down,
           ssd_w_in, ssd_conv_w, ssd_conv_b, ssd_dt_bias, ssd_a_log, ssd_d, ssd_norm, ssd_w_out,
           swa_w_qkv, swa_sink, swa_w_o,
           gdn_w_in, gdn_conv_w, gdn_dt_bias, gdn_a_log, gdn_norm, gdn_w_out,
           nsa_w_in, nsa_pos_k, nsa_pos_v, nsa_cmp_w1_k, nsa_cmp_w2_k, nsa_cmp_w1_v, nsa_cmp_w2_v, nsa_w_o):
    batch, seq, d = x_prompt.shape
    nb = x_sample.shape[0]
    depth = norm_mix.shape[0]
    past = page_table.shape[1] * PAGE_SIZE
    xp = x_prompt.reshape(batch * seq, d)
    xs = x_sample.reshape(nb, d)
    outs_p = {}
    outs_s = {}

    def put(store, name, value):
        store.setdefault(name, []).append(value)

    w1 = ffn_weights(ffn1_gate, ffn1_up, ffn1_down)
    w2 = ffn_weights(ffn2_gate, ffn2_up, ffn2_down)
    for i in range(depth):
        kind, li = i % 4, i // 4
        xp = ffn(xp, norm_ffn1[i], *w1, i)
        xs = ffn(xs, norm_ffn1[i], *w1, i)
        if kind == 0:
            xp, xs, cp, sp, cs, ss = ssd_layer(
                xp, xs, batch, seq, norm_mix[i], ssd_w_in[li], ssd_conv_w[li], ssd_conv_b[li], ssd_dt_bias[li],
                ssd_a_log[li], ssd_d[li], ssd_norm[li], ssd_w_out[li], state_ssd_conv[li], state_ssd[li])
            put(outs_p, "ssd_conv", cp), put(outs_p, "ssd_state", sp)
            put(outs_s, "ssd_conv", cs), put(outs_s, "ssd_state", ss)
        elif kind == 1:
            xp, xs, kp, vp, kq, vq = swa_layer(xp, xs, batch, seq, norm_mix[i], rel_table, swa_w_qkv[li], swa_sink[li],
                                               swa_w_o[li], cache_swa_k[li], cache_swa_v[li])
            put(outs_p, "swa_k", kp), put(outs_p, "swa_v", vp)
            put(outs_s, "swa_k", kq), put(outs_s, "swa_v", vq)
        elif kind == 2:
            xp, xs, cp, sp, cs, ss = gdn_layer(
                xp, xs, batch, seq, norm_mix[i], gdn_w_in[li], gdn_conv_w[li], gdn_dt_bias[li], gdn_a_log[li],
                gdn_norm[li], gdn_w_out[li], state_gdn_conv[li], state_gdn[li])
            put(outs_p, "gdn_conv", cp), put(outs_p, "gdn_state", sp)
            put(outs_s, "gdn_conv", cs), put(outs_s, "gdn_state", ss)
        else:
            xp, xs, op, os_ = nsa_layer(
                xp, xs, batch, seq, past, norm_mix[i], rel_table, nsa_w_in[li], nsa_pos_k[li], nsa_pos_v[li],
                nsa_cmp_w1_k[li], nsa_cmp_w2_k[li], nsa_cmp_w1_v[li], nsa_cmp_w2_v[li], nsa_w_o[li],
                cache_nsa_cmp_k[li], cache_nsa_cmp_v[li], cache_nsa_sel_k[li], cache_nsa_sel_v[li],
                cache_nsa_win_k[li], cache_nsa_win_v[li], page_table)
            for j in range(6):
                put(outs_p, f"nsa{j}", op[j]), put(outs_s, f"nsa{j}", os_[j])
        xp = ffn(xp, norm_ffn2[i], *w2, i)
        xs = ffn(xs, norm_ffn2[i], *w2, i)

    y_prompt = final_norm(xp, norm_final).reshape(batch, seq, d)
    y_sample = final_norm(xs, norm_final).reshape(nb, 1, d)
    order = ["ssd_conv", "ssd_state", "swa_k", "swa_v", "gdn_conv", "gdn_state"] + [f"nsa{j}" for j in range(6)]
    states_p = tuple(jnp.stack(outs_p[name]) for name in order)
    states_s = tuple(jnp.stack(outs_s[name]) for name in order)
    return (y_prompt, y_sample) + states_p + states_s
```

```python
import functools
import math

import jax
import jax.numpy as jnp
from jax import lax
from jax.experimental import pallas as pl
from jax.experimental.pallas import tpu as pltpu

F32 = jnp.float32
BF16 = jnp.bfloat16
HIGHEST = lax.Precision.HIGHEST

EPS = 1e-6
D_MODEL = 2048
D_FF = 5504
LANE = 128
VMEM_LIMIT = 56 * 1024 * 1024
NEG = -1e30

FF_TILE = 512
D_FF_PAD = -(-D_FF // FF_TILE) * FF_TILE
ROW_TILE = 1024
FFN_ROW_TILE = 1024
COL_TILE = 512

SSD_INNER = 4096
SSD_HEADS = 64
SSD_P = 64
SSD_N = 128
SSD_GROUPS = 8
SSD_GW = SSD_INNER // SSD_GROUPS
SSD_CONV_DIM = SSD_INNER + 2 * SSD_GROUPS * SSD_N
SSD_CHUNK = 128


def _cparams(*sem):
    return pltpu.CompilerParams(dimension_semantics=sem, vmem_limit_bytes=VMEM_LIMIT)


def _pad_cols(w, n):
    return jnp.pad(w, ((0, 0), (0, n - w.shape[1])))


def _round_up(n, m):
    return -(-n // m) * m


def _sigmoid(x):
    return 1.0 / (1.0 + jnp.exp(-x))


def _silu(x):
    return x * _sigmoid(x)


def _softplus(x):
    return jnp.maximum(x, 0.0) + jnp.log(1.0 + jnp.exp(-jnp.abs(x)))


def _rms(x, g):
    return x * lax.rsqrt(jnp.mean(x * x, axis=-1, keepdims=True) + EPS) * g


def _dot(a, b, **kw):
    return jnp.dot(a, b, preferred_element_type=F32, **kw)


def _dot_nt(a, b):
    return lax.dot_general(a, b, (((1,), (1,)), ((), ())), preferred_element_type=F32)


def _dot_tn(a, b):
    return lax.dot_general(a, b, (((0,), (0,)), ((), ())), preferred_element_type=F32)


def _eye(n):
    r = lax.broadcasted_iota(jnp.int32, (n, n), 0)
    c = lax.broadcasted_iota(jnp.int32, (n, n), 1)
    return r == c


def _row_to_col(v, eye):
    n = v.shape[1]
    return jnp.sum(jnp.where(eye, jnp.broadcast_to(v, (n, n)), 0.0), axis=-1, keepdims=True)


def _col_to_row(v, eye):
    n = v.shape[0]
    return jnp.sum(jnp.where(eye, jnp.broadcast_to(v, (n, n)), 0.0), axis=0, keepdims=True)


def _ffn_kernel(x_ref, g_ref, wg_ref, wu_ref, wd_ref, o_ref, xn_ref):
    @pl.when(pl.program_id(1) == 0)
    def _():
        x = x_ref[...]
        xn_ref[...] = _rms(x, g_ref[...]).astype(BF16)
        o_ref[...] = x

    xn = xn_ref[...]
    gate = _dot(xn, wg_ref[...])
    up = _dot(xn, wu_ref[...])
    h = (_silu(gate) * (0.5 * up)).astype(BF16)
    o_ref[...] += _dot(h, wd_ref[...])


def ffn(x, g, wg, wu, wd, layer):
    m, d = x.shape
    tm = min(FFN_ROW_TILE, m)
    return pl.pallas_call(
        _ffn_kernel,
        grid=(m // tm, D_FF_PAD // FF_TILE),
        in_specs=[
            pl.BlockSpec((tm, d), lambda i, f: (i, 0)),
            pl.BlockSpec((1, d), lambda i, f: (0, 0)),
            pl.BlockSpec((None, d, FF_TILE), lambda i, f: (layer, 0, f)),
            pl.BlockSpec((None, d, FF_TILE), lambda i, f: (layer, 0, f)),
            pl.BlockSpec((None, FF_TILE, d), lambda i, f: (layer, f, 0)),
        ],
        out_specs=pl.BlockSpec((tm, d), lambda i, f: (i, 0)),
        out_shape=jax.ShapeDtypeStruct((m, d), F32),
        scratch_shapes=[pltpu.VMEM((tm, d), BF16)],
        compiler_params=_cparams("parallel", "arbitrary"),
        name="ffn",
    )(x, g.reshape(1, d), wg, wu, wd)


def _norm_proj_kernel(x_ref, g_ref, w_ref, o_ref, xn_ref):
    @pl.when(pl.program_id(1) == 0)
    def _():
        xn_ref[...] = _rms(x_ref[...], g_ref[...]).astype(BF16)

    o_ref[...] = _dot(xn_ref[...], w_ref[...])


def norm_proj(x, g, w):
    m, d = x.shape
    n = w.shape[1]
    tm = min(ROW_TILE, m)
    return pl.pallas_call(
        _norm_proj_kernel,
        grid=(m // tm, n // COL_TILE),
        in_specs=[
            pl.BlockSpec((tm, d), lambda i, j: (i, 0)),
            pl.BlockSpec((1, d), lambda i, j: (0, 0)),
            pl.BlockSpec((d, COL_TILE), lambda i, j: (0, j)),
        ],
        out_specs=pl.BlockSpec((tm, COL_TILE), lambda i, j: (i, j)),
        out_shape=jax.ShapeDtypeStruct((m, n), F32),
        scratch_shapes=[pltpu.VMEM((tm, d), BF16)],
        compiler_params=_cparams("parallel", "arbitrary"),
        name="norm_proj",
    )(x, g.reshape(1, d), w)


def _proj_res_kernel(a_ref, w_ref, r_ref, o_ref):
    o_ref[...] = r_ref[...] + _dot(a_ref[...], w_ref[...])


def proj_res(a, w, res):
    m, k = a.shape
    n = w.shape[1]
    tm = min(ROW_TILE, m)
    return pl.pallas_call(
        _proj_res_kernel,
        grid=(m // tm, n // COL_TILE),
        in_specs=[
            pl.BlockSpec((tm, k), lambda i, j: (i, 0)),
            pl.BlockSpec((k, COL_TILE), lambda i, j: (0, j)),
            pl.BlockSpec((tm, COL_TILE), lambda i, j: (i, j)),
        ],
        out_specs=pl.BlockSpec((tm, COL_TILE), lambda i, j: (i, j)),
        out_shape=jax.ShapeDtypeStruct((m, n), F32),
        compiler_params=_cparams("parallel", "parallel"),
        name="proj_res",
    )(a, w, res)


def _final_norm_kernel(x_ref, g_ref, o_ref):
    o_ref[...] = _rms(x_ref[...], g_ref[...])


def final_norm(x, g):
    m, d = x.shape
    tm = min(ROW_TILE, m)
    return pl.pallas_call(
        _final_norm_kernel,
        grid=(m // tm,),
        in_specs=[pl.BlockSpec((tm, d), lambda i: (i, 0)), pl.BlockSpec((1, d), lambda i: (0, 0))],
        out_specs=pl.BlockSpec((tm, d), lambda i: (i, 0)),
        out_shape=jax.ShapeDtypeStruct((m, d), F32),
        compiler_params=_cparams("parallel"),
        name="final_norm",
    )(x, g.reshape(1, d))


def _head_expand(width):
    r = lax.broadcasted_iota(jnp.int32, (LANE, width), 0)
    c = lax.broadcasted_iota(jnp.int32, (LANE, width), 1)
    return jnp.where(lax.shift_right_logical(c, 6) == r, 1.0, 0.0).astype(F32)


def _ssd_prompt_kernel(z_ref, x_ref, bc_ref, dt_ref, cw_ref, cb_ref, dtb_ref, alog_ref, dx_ref, ng_ref,
                       y_ref, st_ref, xpad_ref, s_ref, ybuf_ref):
    c = pl.program_id(1)
    L = SSD_CHUNK

    @pl.when(c == 0)
    def _():
        xpad_ref[0:8, :] = jnp.zeros((8, SSD_CONV_DIM), F32)
        s_ref[...] = jnp.zeros_like(s_ref)

    xpad_ref[8:8 + L, 0:SSD_INNER] = x_ref[...]
    xpad_ref[8:8 + L, SSD_INNER:SSD_CONV_DIM] = bc_ref[...]
    conv = cb_ref[...] + cw_ref[3:4, :] * xpad_ref[8:8 + L, :]
    for j in range(3):
        conv = conv + cw_ref[j:j + 1, :] * xpad_ref[pl.ds(5 + j, L), :]
    xpad_ref[0:8, :] = xpad_ref[L:L + 8, :]
    act = _silu(conv)
    xs = act[:, 0:SSD_INNER]
    bm = act[:, SSD_INNER:SSD_INNER + SSD_GROUPS * SSD_N]
    cm = act[:, SSD_INNER + SSD_GROUPS * SSD_N:SSD_CONV_DIM]

    dt = _softplus(dt_ref[...] + dtb_ref[...])
    da = dt * (-jnp.exp(alog_ref[...]))
    row = lax.broadcasted_iota(jnp.int32, (L, L), 0)
    col = lax.broadcasted_iota(jnp.int32, (L, L), 1)
    causal = row >= col
    cs = _dot(jnp.where(causal, 1.0, 0.0).astype(F32), da, precision=HIGHEST)
    cs_t = cs.T
    expand = _head_expand(SSD_INNER)
    dtx = _dot(dt, expand, precision=HIGHEST)
    csx = _dot(cs, expand, precision=HIGHEST)
    xdt = xs * dtx
    cs_last = csx[L - 1:L, :]
    xdt_end = (xdt * jnp.exp(cs_last - csx)).astype(BF16)
    xdt_b = xdt.astype(BF16)
    ecs = jnp.exp(csx)
    dec_last = jnp.exp(cs_last)
    lane = lax.broadcasted_iota(jnp.int32, (L, LANE), 1)

    for g in range(SSD_GROUPS):
        lo_n = g * SSD_N
        lo_f = g * SSD_GW
        bg = bm[:, lo_n:lo_n + SSD_N]
        cg = cm[:, lo_n:lo_n + SSD_N].astype(BF16)
        cb = _dot_nt(cg, bg.astype(BF16))
        sg = s_ref[:, lo_f:lo_f + SSD_GW]
        yoff = _dot(cg, sg.astype(BF16)) * ecs[:, lo_f:lo_f + SSD_GW]
        for pr in range(SSD_GW // LANE):
            lo = lo_f + pr * LANE
            xp = xdt_b[:, lo:lo + LANE]
            outs = []
            for hh in (g * 8 + 2 * pr, g * 8 + 2 * pr + 1):
                diff = cs[:, hh:hh + 1] - cs_t[hh:hh + 1, :]
                dec = jnp.exp(jnp.where(causal, diff, NEG))
                outs.append(_dot((cb * dec).astype(BF16), xp))
            ybuf_ref[:, lo:lo + LANE] = jnp.where(lane < SSD_P, outs[0], outs[1]) + yoff[:, pr * LANE:(pr + 1) * LANE]
        s_ref[:, lo_f:lo_f + SSD_GW] = (sg * dec_last[:, lo_f:lo_f + SSD_GW]
                                        + _dot(bg.T.astype(BF16), xdt_end[:, lo_f:lo_f + SSD_GW]))

    y = ybuf_ref[...] + xs * dx_ref[...]
    yz = y * _silu(z_ref[...])
    for g in range(SSD_GROUPS):
        lo_f = g * SSD_GW
        y_ref[:, lo_f:lo_f + SSD_GW] = _rms(yz[:, lo_f:lo_f + SSD_GW], ng_ref[:, lo_f:lo_f + SSD_GW]).astype(BF16)

    @pl.when(c == pl.num_programs(1) - 1)
    def _():
        for k in range(SSD_INNER // LANE):
            st_ref[0, k * LANE:(k + 1) * LANE, :] = s_ref[:, k * LANE:(k + 1) * LANE].T


def _pad_lanes(v, n=LANE):
    v = v.reshape(1, -1).astype(F32)
    return jnp.pad(v, ((0, 0), (0, n - v.shape[1])))


def _pad_lanes_at(v, start, n=LANE):
    v = v.reshape(1, -1).astype(F32)
    return jnp.pad(v, ((0, 0), (start, n - start - v.shape[1])))


def ssd_prompt(proj, batch, seq, conv_w, conv_b, dt_bias, a_log, d_skip, norm_g):
    L = SSD_CHUNK
    nc = seq // L
    rowb = lambda b, c: b * nc + c
    full = lambda shape: pl.BlockSpec(shape, lambda b, c: (0,) * len(shape))
    y, st = pl.pallas_call(
        _ssd_prompt_kernel,
        grid=(batch, nc),
        in_specs=[
            pl.BlockSpec((L, SSD_INNER), lambda b, c: (rowb(b, c), 0)),
            pl.BlockSpec((L, SSD_INNER), lambda b, c: (rowb(b, c), 1)),
            pl.BlockSpec((L, 2 * SSD_GROUPS * SSD_N), lambda b, c: (rowb(b, c), 4)),
            pl.BlockSpec((L, LANE), lambda b, c: (rowb(b, c), (SSD_INNER + SSD_CONV_DIM) // LANE)),
            full((4, SSD_CONV_DIM)), full((1, SSD_CONV_DIM)), full((1, LANE)), full((1, LANE)),
            full((1, SSD_INNER)), full((1, SSD_INNER)),
        ],
        out_specs=[
            pl.BlockSpec((L, SSD_INNER), lambda b, c: (rowb(b, c), 0)),
            pl.BlockSpec((1, SSD_INNER, SSD_N), lambda b, c: (b, 0, 0)),
        ],
        out_shape=[
            jax.ShapeDtypeStruct((batch * seq, SSD_INNER), BF16),
            jax.ShapeDtypeStruct((batch, SSD_INNER, SSD_N), F32),
        ],
        scratch_shapes=[
            pltpu.VMEM((L + 8, SSD_CONV_DIM), F32),
            pltpu.VMEM((SSD_N, SSD_INNER), F32),
            pltpu.VMEM((L, SSD_INNER), F32),
        ],
        compiler_params=_cparams("parallel", "arbitrary"),
        name="ssd_prompt",
    )(proj, proj, proj, proj, conv_w, conv_b.reshape(1, -1), _pad_lanes(dt_bias), _pad_lanes(a_log),
      jnp.repeat(d_skip, SSD_P).reshape(1, -1), norm_g.reshape(1, -1))
    return y, st.reshape(batch, SSD_HEADS, SSD_P, SSD_N)


def _ssd_sample_pre_kernel(x_ref, bc_ref, dt_ref, prev_ref, cw_ref, cb_ref, dtb_ref, alog_ref,
                           xs_ref, b_ref, c_ref, xdt_ref, dec_ref):
    xnew = jnp.concatenate([x_ref[...], bc_ref[...]], axis=1)
    conv = cb_ref[...] + cw_ref[3:4, :] * xnew
    for j in range(3):
        conv = conv + cw_ref[j:j + 1, :] * prev_ref[j]
    act = _silu(conv)
    xs = act[:, 0:SSD_INNER]
    dt = _softplus(dt_ref[...] + dtb_ref[...])
    da = dt * (-jnp.exp(alog_ref[...]))
    expand = _head_expand(SSD_INNER)
    xs_ref[...] = xs
    b_ref[...] = act[:, SSD_INNER:SSD_INNER + SSD_GROUPS * SSD_N]
    c_ref[...] = act[:, SSD_INNER + SSD_GROUPS * SSD_N:SSD_CONV_DIM]
    xdt_ref[...] = xs * _dot(dt, expand, precision=HIGHEST)
    dec_ref[...] = jnp.exp(_dot(da, expand, precision=HIGHEST))


def _ssd_sample_state_kernel(s_ref, xdt_ref, dec_ref, b_ref, c_ref, so_ref, y_ref):
    eye = _eye(LANE)
    for k in range(SSD_INNER // LANE):
        g = k // (SSD_GW // LANE)
        lo = k * LANE
        xcol = _row_to_col(xdt_ref[0, :, lo:lo + LANE], eye)
        dcol = _row_to_col(dec_ref[0, :, lo:lo + LANE], eye)
        brow = b_ref[0, :, g * SSD_N:(g + 1) * SSD_N]
        crow = c_ref[0, :, g * SSD_N:(g + 1) * SSD_N]
        snew = dcol * s_ref[0, lo:lo + LANE, :] + xcol * brow
        so_ref[0, lo:lo + LANE, :] = snew
        ycol = jnp.sum(snew * crow, axis=-1, keepdims=True)
        y_ref[0, :, lo:lo + LANE] = _col_to_row(ycol, eye)


def _ssd_sample_post_kernel(y_ref, xs_ref, z_ref, dx_ref, ng_ref, o_ref):
    y = y_ref[...] + xs_ref[...] * dx_ref[...]
    yz = y * _silu(z_ref[...])
    for g in range(SSD_GROUPS):
        lo_f = g * SSD_GW
        o_ref[:, lo_f:lo_f + SSD_GW] = _rms(yz[:, lo_f:lo_f + SSD_GW], ng_ref[:, lo_f:lo_f + SSD_GW]).astype(BF16)


def ssd_sample(proj, conv_prev, state, conv_w, conv_b, dt_bias, a_log, d_skip, norm_g):
    nb = proj.shape[0]
    full = lambda shape: pl.BlockSpec(shape, lambda *_: (0,) * len(shape))
    row_spec = lambda width, blk: pl.BlockSpec((nb, width), lambda *_: (0, blk))
    sds = lambda width: jax.ShapeDtypeStruct((nb, width), F32)
    gn = SSD_GROUPS * SSD_N
    xs, bm, cm, xdt, dec = pl.pallas_call(
        _ssd_sample_pre_kernel,
        grid=(1,),
        in_specs=[
            row_spec(SSD_INNER, 1), row_spec(2 * gn, 4), row_spec(LANE, (SSD_INNER + SSD_CONV_DIM) // LANE),
            full((3, nb, SSD_CONV_DIM)), full((4, SSD_CONV_DIM)), full((1, SSD_CONV_DIM)), full((1, LANE)), full((1, LANE)),
        ],
        out_specs=[full((nb, SSD_INNER)), full((nb, gn)), full((nb, gn)), full((nb, SSD_INNER)), full((nb, SSD_INNER))],
        out_shape=[sds(SSD_INNER), sds(gn), sds(gn), sds(SSD_INNER), sds(SSD_INNER)],
        compiler_params=_cparams("arbitrary"),
        name="ssd_sample_pre",
    )(proj, proj, proj, jnp.swapaxes(conv_prev, 0, 1), conv_w, conv_b.reshape(1, -1), _pad_lanes(dt_bias), _pad_lanes(a_log))
    per_row = lambda width: pl.BlockSpec((1, 1, width), lambda b: (b, 0, 0))
    s_new, y = pl.pallas_call(
        _ssd_sample_state_kernel,
        grid=(nb,),
        in_specs=[
            pl.BlockSpec((1, SSD_INNER, SSD_N), lambda b: (b, 0, 0)),
            per_row(SSD_INNER), per_row(SSD_INNER), per_row(gn), per_row(gn),
        ],
        out_specs=[pl.BlockSpec((1, SSD_INNER, SSD_N), lambda b: (b, 0, 0)), per_row(SSD_INNER)],
        out_shape=[jax.ShapeDtypeStruct((nb, SSD_INNER, SSD_N), F32), jax.ShapeDtypeStruct((nb, 1, SSD_INNER), F32)],
        compiler_params=_cparams("parallel"),
        name="ssd_sample_state",
    )(state.reshape(nb, SSD_INNER, SSD_N), xdt.reshape(nb, 1, -1), dec.reshape(nb, 1, -1),
      bm.reshape(nb, 1, -1), cm.reshape(nb, 1, -1))
    y = y.reshape(nb, SSD_INNER)
    out = pl.pallas_call(
        _ssd_sample_post_kernel,
        grid=(1,),
        in_specs=[full((nb, SSD_INNER)), full((nb, SSD_INNER)), row_spec(SSD_INNER, 0), full((1, SSD_INNER)), full((1, SSD_INNER))],
        out_specs=full((nb, SSD_INNER)),
        out_shape=jax.ShapeDtypeStruct((nb, SSD_INNER), BF16),
        compiler_params=_cparams("arbitrary"),
        name="ssd_sample_post",
    )(y, xs, proj, jnp.repeat(d_skip, SSD_P).reshape(1, -1), norm_g.reshape(1, -1))
    return out, s_new.reshape(nb, SSD_HEADS, SSD_P, SSD_N)


def ssd_weights(w_in):
    return proj_weights(w_in)


def ssd_layer(xp, xs, batch, seq, norm_g_mix, w_in, conv_w, conv_b, dt_bias, a_log, d_skip, norm_g, w_out,
              conv_state, ssm_state):
    w_in_b = ssd_weights(w_in)
    w_out_b = w_out.astype(BF16)
    args = (conv_w, conv_b, dt_bias, a_log, d_skip, norm_g)
    proj_p = norm_proj(xp, norm_g_mix, w_in_b)
    y_p, st_p = ssd_prompt(proj_p, batch, seq, *args)
    xp = proj_res(y_p, w_out_b, xp)
    raw_p = proj_p.reshape(batch, seq, -1)[:, seq - 3:, SSD_INNER:SSD_INNER + SSD_CONV_DIM]
    proj_s = norm_proj(xs, norm_g_mix, w_in_b)
    y_s, st_s = ssd_sample(proj_s, conv_state, ssm_state, *args)
    xs = proj_res(y_s, w_out_b, xs)
    raw_s = jnp.concatenate([conv_state[:, 1:], proj_s[:, None, SSD_INNER:SSD_INNER + SSD_CONV_DIM]], axis=1)
    return xp, xs, raw_p, st_p, raw_s, st_s


CAST_ROWS = 128
CAST_WIDE = 6144


def proj_weights(w_in):
    d, n = w_in.shape
    return cast_pad(w_in[None], d, _round_up(n, COL_TILE))[0]


def _cast_pad_kernel(x_ref, o_ref, *, row_tiles):
    o_ref[...] = jnp.zeros_like(o_ref)

    @pl.when(pl.program_id(1) < row_tiles)
    def _():
        o_ref[:, 0:x_ref.shape[1]] = x_ref[...].astype(BF16)


def cast_pad(w, rows, cols):
    depth, r, c = w.shape
    tile = 4 * CAST_ROWS if (rows == r and c <= CAST_WIDE) else CAST_ROWS
    row_tiles = r // tile
    return pl.pallas_call(
        functools.partial(_cast_pad_kernel, row_tiles=row_tiles),
        grid=(depth, rows // tile),
        in_specs=[pl.BlockSpec((None, tile, c), lambda l, i: (l, jnp.minimum(i, row_tiles - 1), 0))],
        out_specs=pl.BlockSpec((None, tile, cols), lambda l, i: (l, i, 0)),
        out_shape=jax.ShapeDtypeStruct((depth, rows, cols), BF16),
        compiler_params=_cparams("parallel", "parallel"),
        name="cast_pad",
    )(w)


def ffn_weights(w_gate, w_up, w_down):
    d = w_gate.shape[1]
    return cast_pad(w_gate, d, D_FF_PAD), cast_pad(w_up, d, D_FF_PAD), cast_pad(w_down, D_FF_PAD, d)


HEAD_DIM = 128
ATT_HEADS = 16
ATT_BLOCK = 128
REL_BUCKETS = 32
REL_MAX_DIST = 128


def rel_bucket(dist):
    exact = REL_BUCKETS // 2
    d = jnp.maximum(dist, 0)
    logd = jnp.log(jnp.maximum(d, 1).astype(F32) / exact)
    far = exact + (logd / math.log(REL_MAX_DIST / exact) * (REL_BUCKETS - exact)).astype(jnp.int32)
    return jnp.where(d < exact, d, jnp.minimum(far, REL_BUCKETS - 1))


def head_bias(table, dist):
    onehot = (rel_bucket(dist)[..., None] == jnp.arange(REL_BUCKETS)).astype(F32)
    out = jnp.dot(onehot, table.astype(F32), precision=HIGHEST)
    return jnp.moveaxis(out, -1, 0)


def local_bias(table, n_prev, window, groups):
    width = (n_prev + 1) * ATT_BLOCK
    dist = n_prev * ATT_BLOCK + jnp.arange(ATT_BLOCK)[:, None] - jnp.arange(width)[None, :]
    bias = jnp.where((dist >= 0) & (dist <= window), head_bias(table, dist), NEG)
    return bias.reshape(groups, (ATT_HEADS // groups) * ATT_BLOCK, width)


def _banded_kernel(q_ref, k_ref, v_ref, bias_ref, sink_ref, o_ref, *, groups, n_prev, use_sink):
    qi = pl.program_id(1)
    rep = ATT_HEADS // groups
    scale = HEAD_DIM ** -0.5
    for g in range(groups):
        qs = jnp.concatenate(
            [q_ref[:, (g * rep + r) * HEAD_DIM:(g * rep + r + 1) * HEAD_DIM] for r in range(rep)], axis=0)
        qs = (qs * scale).astype(BF16)
        logits = []
        vals = []
        for s in range(n_prev + 1):
            kb = qi - n_prev + s
            start = pl.multiple_of(jnp.maximum(kb, 0) * ATT_BLOCK, ATT_BLOCK)
            kblk = k_ref[pl.ds(start, ATT_BLOCK), g * HEAD_DIM:(g + 1) * HEAD_DIM].astype(BF16)
            vals.append(v_ref[pl.ds(start, ATT_BLOCK), g * HEAD_DIM:(g + 1) * HEAD_DIM].astype(BF16))
            l_s = _dot_nt(qs, kblk) + bias_ref[g, :, s * ATT_BLOCK:(s + 1) * ATT_BLOCK]
            logits.append(jnp.where(kb >= 0, l_s, NEG))
        m = functools.reduce(jnp.maximum, [jnp.max(l, axis=-1, keepdims=True) for l in logits])
        if use_sink:
            sink = sink_ref[g][:, 0:1]
            m = jnp.maximum(m, sink)
            denom = jnp.exp(sink - m)
        else:
            denom = jnp.zeros_like(m)
        acc = jnp.zeros((rep * ATT_BLOCK, HEAD_DIM), F32)
        for l_s, vblk in zip(logits, vals):
            p = jnp.exp(l_s - m)
            denom = denom + jnp.sum(p, axis=-1, keepdims=True)
            acc = acc + _dot(p.astype(BF16), vblk)
        out = acc / denom
        for r in range(rep):
            h = g * rep + r
            o_ref[:, h * HEAD_DIM:(h + 1) * HEAD_DIM] = out[r * ATT_BLOCK:(r + 1) * ATT_BLOCK, :].astype(o_ref.dtype)


def banded_attention(proj, batch, seq, q_blk, k_blk, v_blk, groups, n_prev, bias, sink_rows, out_dtype):
    nq = seq // ATT_BLOCK
    kvw = groups * HEAD_DIM
    rep = ATT_HEADS // groups
    use_sink = sink_rows is not None
    if sink_rows is None:
        sink_rows = jnp.zeros((groups, rep * ATT_BLOCK, LANE), F32)
    return pl.pallas_call(
        functools.partial(_banded_kernel, groups=groups, n_prev=n_prev, use_sink=use_sink),
        grid=(batch, nq),
        in_specs=[
            pl.BlockSpec((ATT_BLOCK, ATT_HEADS * HEAD_DIM), lambda b, i: (b * nq + i, q_blk)),
            pl.BlockSpec((seq, kvw), lambda b, i: (b, k_blk)),
            pl.BlockSpec((seq, kvw), lambda b, i: (b, v_blk)),
            pl.BlockSpec(bias.shape, lambda b, i: (0, 0, 0)),
            pl.BlockSpec(sink_rows.shape, lambda b, i: (0, 0, 0)),
        ],
        out_specs=pl.BlockSpec((ATT_BLOCK, ATT_HEADS * HEAD_DIM), lambda b, i: (b * nq + i, 0)),
        out_shape=jax.ShapeDtypeStruct((batch * seq, ATT_HEADS * HEAD_DIM), out_dtype),
        compiler_params=_cparams("parallel", "arbitrary"),
        name="banded_attention",
    )(proj, proj, proj, bias, sink_rows)


def _decode_kernel(q_ref, kn_ref, vn_ref, kb_ref, vb_ref, bias_ref, bnew_ref, sink_ref, o_ref, *, groups, use_sink):
    rep = ATT_HEADS // groups
    scale = HEAD_DIM ** -0.5
    kall = kb_ref[0].astype(BF16)
    vall = vb_ref[0].astype(BF16)
    row_head = lax.rem(lax.broadcasted_iota(jnp.int32, (1, kall.shape[0]), 1), groups)
    for g in range(groups):
        qg = jnp.concatenate(
            [q_ref[0, :, (g * rep + r) * HEAD_DIM:(g * rep + r + 1) * HEAD_DIM] for r in range(rep)], axis=0)
        qg = qg * scale
        knew = kn_ref[0, :, g * HEAD_DIM:(g + 1) * HEAD_DIM]
        vnew = vn_ref[0, :, g * HEAD_DIM:(g + 1) * HEAD_DIM]
        l_c = jnp.where(row_head == g, _dot_nt(qg.astype(BF16), kall) + bias_ref[g * rep:(g + 1) * rep, :], NEG)
        l_n = jnp.sum(qg * knew, axis=-1, keepdims=True) + bnew_ref[g * rep:(g + 1) * rep, 0:1]
        m = jnp.maximum(jnp.max(l_c, axis=-1, keepdims=True), l_n)
        if use_sink:
            sink = sink_ref[g * rep:(g + 1) * rep, 0:1]
            m = jnp.maximum(m, sink)
            denom = jnp.exp(sink - m)
        else:
            denom = jnp.zeros_like(m)
        p_c = jnp.exp(l_c - m)
        p_n = jnp.exp(l_n - m)
        denom = denom + jnp.sum(p_c, axis=-1, keepdims=True) + p_n
        out = (_dot(p_c.astype(BF16), vall) + p_n * vnew) / denom
        for r in range(rep):
            h = g * rep + r
            o_ref[0, :, h * HEAD_DIM:(h + 1) * HEAD_DIM] = out[r:r + 1, :].astype(o_ref.dtype)


def decode_attention(proj, q_blk, k_blk, v_blk, k_buf, v_buf, groups, table, sink, out_dtype):
    nb, wn = k_buf.shape[0], k_buf.shape[1]
    kvw = groups * HEAD_DIM
    bias = jnp.repeat(head_bias(table, wn - jnp.arange(wn)), groups, axis=1)
    bnew = jnp.broadcast_to(table[0][:, None].astype(F32), (ATT_HEADS, LANE))
    use_sink = sink is not None
    sink_rows = jnp.broadcast_to((sink if use_sink else jnp.zeros((ATT_HEADS,), F32))[:, None].astype(F32), (ATT_HEADS, LANE))
    full = lambda shape: pl.BlockSpec(shape, lambda b: (0,) * len(shape))
    proj3 = proj.reshape(nb, 1, -1)
    out = pl.pallas_call(
        functools.partial(_decode_kernel, groups=groups, use_sink=use_sink),
        grid=(nb,),
        in_specs=[
            pl.BlockSpec((1, 1, ATT_HEADS * HEAD_DIM), lambda b: (b, 0, q_blk)),
            pl.BlockSpec((1, 1, kvw), lambda b: (b, 0, k_blk)),
            pl.BlockSpec((1, 1, kvw), lambda b: (b, 0, v_blk)),
            pl.BlockSpec((1, wn * groups, HEAD_DIM), lambda b: (b, 0, 0)),
            pl.BlockSpec((1, wn * groups, HEAD_DIM), lambda b: (b, 0, 0)),
            full((ATT_HEADS, wn * groups)), full((ATT_HEADS, LANE)), full((ATT_HEADS, LANE)),
        ],
        out_specs=pl.BlockSpec((1, 1, ATT_HEADS * HEAD_DIM), lambda b: (b, 0, 0)),
        out_shape=jax.ShapeDtypeStruct((nb, 1, ATT_HEADS * HEAD_DIM), out_dtype),
        compiler_params=_cparams("parallel"),
        name="decode_attention",
    )(proj3, proj3, proj3, k_buf.reshape(nb, wn * groups, HEAD_DIM), v_buf.reshape(nb, wn * groups, HEAD_DIM),
      bias, bnew, sink_rows)
    return out.reshape(nb, ATT_HEADS * HEAD_DIM)


SWA_KV = 4
SWA_WINDOW = 128


def swa_layer(xp, xs, batch, seq, norm_g_mix, table, w_qkv, sink, w_o, k_buf, v_buf):
    w_qkv_b = w_qkv.astype(BF16)
    w_o_b = w_o.astype(BF16)
    kvw = SWA_KV * HEAD_DIM
    qd = ATT_HEADS * HEAD_DIM
    rep = ATT_HEADS // SWA_KV
    n_prev = -(-SWA_WINDOW // ATT_BLOCK)
    bias = local_bias(table, n_prev, SWA_WINDOW, SWA_KV)
    sink_rows = jnp.broadcast_to(jnp.repeat(sink.astype(F32), ATT_BLOCK).reshape(SWA_KV, rep * ATT_BLOCK, 1),
                                 (SWA_KV, rep * ATT_BLOCK, LANE))
    proj_p = norm_proj(xp, norm_g_mix, w_qkv_b)
    o_p = banded_attention(proj_p, batch, seq, 0, qd // kvw, qd // kvw + 1, SWA_KV, n_prev, bias, sink_rows, BF16)
    xp = proj_res(o_p, w_o_b, xp)
    buf = min(SWA_WINDOW, seq)
    kv_p = proj_p.reshape(batch, seq, -1)[:, seq - buf:, qd:]
    k_p = kv_p[..., :kvw].reshape(batch, buf, SWA_KV, HEAD_DIM)
    v_p = kv_p[..., kvw:2 * kvw].reshape(batch, buf, SWA_KV, HEAD_DIM)
    proj_s = norm_proj(xs, norm_g_mix, w_qkv_b)
    o_s = decode_attention(proj_s, 0, qd // kvw, qd // kvw + 1, k_buf, v_buf, SWA_KV, table, sink, BF16)
    xs = proj_res(o_s, w_o_b, xs)
    nb = xs.shape[0]
    k_s = jnp.concatenate([k_buf[:, 1:], proj_s[:, None, qd:qd + kvw].reshape(nb, 1, SWA_KV, HEAD_DIM)], axis=1)
    v_s = jnp.concatenate([v_buf[:, 1:], proj_s[:, None, qd + kvw:qd + 2 * kvw].reshape(nb, 1, SWA_KV, HEAD_DIM)], axis=1)
    return xp, xs, k_p, v_p, k_s, v_s


GDN_K_HEADS = 16
GDN_V_HEADS = 32
GDN_D = 128
GDN_KEY = GDN_K_HEADS * GDN_D
GDN_VAL = GDN_V_HEADS * GDN_D
GDN_CONV_DIM = 2 * GDN_KEY + GDN_VAL
GDN_CHUNK = 64
GDN_BETA_BLK = (GDN_CONV_DIM + GDN_VAL) // LANE
GDN_A_LANE = GDN_V_HEADS
GDN_GROUP = 8


def gdn_weights(w_in):
    return proj_weights(w_in)


def _split_bf16(a):
    hi = a.astype(BF16)
    return hi, (a - hi.astype(F32)).astype(BF16)


def _dot3(a, b):
    ah, al = _split_bf16(a)
    bh, bl = _split_bf16(b)
    return _dot(ah, bh) + (_dot(ah, bl) + _dot(al, bh))


def _unit_lower_inverses(ms):
    n = ms[0].shape[0]
    eye = jnp.where(_eye(n), 1.0, 0.0)
    ps = [-m for m in ms]
    ts = [eye + p for p in ps]
    ps = [_dot3(p, p) for p in ps]
    k = 2
    while k < n:
        if 2 * k >= n:
            ts = [t + _dot3(t, p) for t, p in zip(ts, ps)]
        else:
            both = [_dot3(jnp.concatenate([t, p], axis=0), p) for t, p in zip(ts, ps)]
            ts = [t + b[:n] for t, b in zip(ts, both)]
            ps = [b[n:] for b in both]
        k *= 2
    return ts


def _l2n(x):
    return x * lax.rsqrt(jnp.sum(x * x, axis=-1, keepdims=True) + EPS)


def _gdn_prompt_kernel(qk_ref, v_ref, z_ref, beta_ref, a_ref, cw_ref, dtb_ref, alog_ref, ng_ref,
                       o_ref, st_ref, xpad_ref, s_ref):
    c = pl.program_id(1)
    L = GDN_CHUNK

    @pl.when(c == 0)
    def _():
        xpad_ref[0:8, :] = jnp.zeros((8, GDN_CONV_DIM), F32)
        s_ref[...] = jnp.zeros_like(s_ref)

    xpad_ref[8:8 + L, 0:2 * GDN_KEY] = qk_ref[...]
    xpad_ref[8:8 + L, 2 * GDN_KEY:GDN_CONV_DIM] = v_ref[...]
    conv = cw_ref[3:4, :] * xpad_ref[8:8 + L, :]
    for j in range(3):
        conv = conv + cw_ref[j:j + 1, :] * xpad_ref[pl.ds(5 + j, L), :]
    xpad_ref[0:8, :] = xpad_ref[L:L + 8, :]
    act = _silu(conv)

    beta = _sigmoid(beta_ref[...])
    g = -jnp.exp(alog_ref[...]) * _softplus(a_ref[...] + dtb_ref[...])
    row = lax.broadcasted_iota(jnp.int32, (L, L), 0)
    col = lax.broadcasted_iota(jnp.int32, (L, L), 1)
    incl = row >= col
    strict = row > col
    gc = _dot(jnp.where(incl, 1.0, 0.0).astype(F32), g, precision=HIGHEST)
    gc_t = gc.T
    eg = jnp.exp(gc)
    g_last = gc[L - 1:L, :]
    k_scale = jnp.exp(g_last - gc)
    s_scale = jnp.exp(g_last)
    ng = ng_ref[...]

    rep = GDN_V_HEADS // GDN_K_HEADS
    for grp in range(GDN_V_HEADS // GDN_GROUP):
        vhs = list(range(grp * GDN_GROUP, (grp + 1) * GDN_GROUP))
        khs = list(range(vhs[0] // rep, vhs[-1] // rep + 1))
        qn = {kh: _l2n(act[:, kh * GDN_D:(kh + 1) * GDN_D]) * GDN_D ** -0.5 for kh in khs}
        kn = {kh: _l2n(act[:, GDN_KEY + kh * GDN_D:GDN_KEY + (kh + 1) * GDN_D]) for kh in khs}
        kn_b = {kh: kn[kh].astype(BF16) for kh in khs}
        prods = {kh: _dot_nt(jnp.concatenate([kn_b[kh], qn[kh].astype(BF16)], axis=0), kn_b[kh]) for kh in khs}
        bcol = {vh: beta[:, vh:vh + 1] for vh in vhs}
        al = {vh: GDN_A_LANE + vh for vh in vhs}
        egcol = {vh: eg[:, al[vh]:al[vh] + 1] for vh in vhs}
        decay = {vh: jnp.exp(jnp.where(incl, gc[:, al[vh]:al[vh] + 1] - gc_t[al[vh]:al[vh] + 1, :], NEG)) for vh in vhs}
        t_inv = _unit_lower_inverses([jnp.where(strict, prods[vh // rep][:L] * bcol[vh] * decay[vh], 0.0) for vh in vhs])
        rhs = [jnp.concatenate([act[:, 2 * GDN_KEY + vh * GDN_D:2 * GDN_KEY + (vh + 1) * GDN_D] * bcol[vh],
                                kn[vh // rep] * (bcol[vh] * egcol[vh])], axis=1).astype(BF16) for vh in vhs]
        uw = [_dot(t.astype(BF16), r) for t, r in zip(t_inv, rhs)]
        s_old = [s_ref[vh] for vh in vhs]
        ws_qs = [_dot(jnp.concatenate([uw_h[:, GDN_D:], qn[vh // rep] * egcol[vh]], axis=0).astype(BF16), s.astype(BF16))
                 for vh, uw_h, s in zip(vhs, uw, s_old)]
        v_new = [(uw_h[:, :GDN_D] - wq[:L]).astype(BF16) for uw_h, wq in zip(uw, ws_qs)]
        outs = [wq[L:] + _dot((prods[vh // rep][L:] * decay[vh]).astype(BF16), vn)
                for vh, wq, vn in zip(vhs, ws_qs, v_new)]
        upd = [_dot_tn((kn[vh // rep] * k_scale[:, al[vh]:al[vh] + 1]).astype(BF16), vn) for vh, vn in zip(vhs, v_new)]
        for vh, s, u, o in zip(vhs, s_old, upd, outs):
            s_ref[vh] = s * s_scale[:, al[vh]:al[vh] + 1] + u
            zz = z_ref[:, vh * GDN_D:(vh + 1) * GDN_D]
            o_ref[:, vh * GDN_D:(vh + 1) * GDN_D] = (_rms(o, ng) * _silu(zz)).astype(BF16)

    @pl.when(c == pl.num_programs(1) - 1)
    def _():
        st_ref[0] = s_ref[...]


def gdn_prompt(proj, batch, seq, conv_w, dt_bias, a_log, norm_g):
    L = GDN_CHUNK
    nc = seq // L
    rowb = lambda b, c: b * nc + c
    full = lambda shape: pl.BlockSpec(shape, lambda b, c: (0,) * len(shape))
    return pl.pallas_call(
        _gdn_prompt_kernel,
        grid=(batch, nc),
        in_specs=[
            pl.BlockSpec((L, 2 * GDN_KEY), lambda b, c: (rowb(b, c), 0)),
            pl.BlockSpec((L, GDN_VAL), lambda b, c: (rowb(b, c), 1)),
            pl.BlockSpec((L, GDN_VAL), lambda b, c: (rowb(b, c), 2)),
            pl.BlockSpec((L, LANE), lambda b, c: (rowb(b, c), GDN_BETA_BLK)),
            pl.BlockSpec((L, LANE), lambda b, c: (rowb(b, c), GDN_BETA_BLK)),
            full((4, GDN_CONV_DIM)), full((1, LANE)), full((1, LANE)), full((1, GDN_D)),
        ],
        out_specs=[
            pl.BlockSpec((L, GDN_VAL), lambda b, c: (rowb(b, c), 0)),
            pl.BlockSpec((1, GDN_V_HEADS, GDN_D, GDN_D), lambda b, c: (b, 0, 0, 0)),
        ],
        out_shape=[
            jax.ShapeDtypeStruct((batch * seq, GDN_VAL), BF16),
            jax.ShapeDtypeStruct((batch, GDN_V_HEADS, GDN_D, GDN_D), F32),
        ],
        scratch_shapes=[pltpu.VMEM((L + 8, GDN_CONV_DIM), F32), pltpu.VMEM((GDN_V_HEADS, GDN_D, GDN_D), F32)],
        compiler_params=_cparams("parallel", "arbitrary"),
        name="gdn_prompt",
    )(proj, proj, proj, proj, proj, conv_w, _pad_lanes_at(dt_bias, GDN_A_LANE), _pad_lanes_at(a_log, GDN_A_LANE),
      norm_g.reshape(1, -1))


def _gdn_sample_pre_kernel(qk_ref, v_ref, beta_ref, a_ref, prev_ref, cw_ref, dtb_ref, alog_ref,
                           q_ref, k_ref, vo_ref, bo_ref, eg_ref):
    xnew = jnp.concatenate([qk_ref[...], v_ref[...]], axis=1)
    conv = cw_ref[3:4, :] * xnew
    for j in range(3):
        conv = conv + cw_ref[j:j + 1, :] * prev_ref[j]
    act = _silu(conv)
    for kh in range(GDN_K_HEADS):
        q_ref[:, kh * GDN_D:(kh + 1) * GDN_D] = _l2n(act[:, kh * GDN_D:(kh + 1) * GDN_D]) * GDN_D ** -0.5
        k_ref[:, kh * GDN_D:(kh + 1) * GDN_D] = _l2n(act[:, GDN_KEY + kh * GDN_D:GDN_KEY + (kh + 1) * GDN_D])
    vo_ref[...] = act[:, 2 * GDN_KEY:]
    bo_ref[...] = _sigmoid(beta_ref[...])
    eg_ref[...] = jnp.exp(-jnp.exp(alog_ref[...]) * _softplus(a_ref[...] + dtb_ref[...]))


def _gdn_sample_state_kernel(s_ref, q_ref, k_ref, v_ref, z_ref, beta_ref, eg_ref, ng_ref, so_ref, o_ref):
    eye = _eye(GDN_D)
    ng = ng_ref[...]
    for kh in range(GDN_K_HEADS):
        qrow = q_ref[0, :, kh * GDN_D:(kh + 1) * GDN_D]
        krow = k_ref[0, :, kh * GDN_D:(kh + 1) * GDN_D]
        qcol = _row_to_col(qrow, eye)
        kcol = _row_to_col(krow, eye)
        qk = jnp.sum(qrow * krow, axis=-1, keepdims=True)
        for vh in (2 * kh, 2 * kh + 1):
            s_old = s_ref[0, vh]
            beta = beta_ref[0, :, vh:vh + 1]
            eg = eg_ref[0, :, GDN_A_LANE + vh:GDN_A_LANE + vh + 1]
            v = v_ref[0, :, vh * GDN_D:(vh + 1) * GDN_D]
            ks = jnp.sum(kcol * s_old, axis=0, keepdims=True)
            qs = jnp.sum(qcol * s_old, axis=0, keepdims=True)
            v_new = beta * v - (beta * eg) * ks
            o = eg * qs + qk * v_new
            so_ref[0, vh] = s_old * eg + kcol * v_new
            zz = z_ref[0, :, vh * GDN_D:(vh + 1) * GDN_D]
            o_ref[0, :, vh * GDN_D:(vh + 1) * GDN_D] = (_rms(o, ng) * _silu(zz)).astype(BF16)


def gdn_sample(proj, conv_prev, state, conv_w, dt_bias, a_log, norm_g):
    nb = proj.shape[0]
    full = lambda shape: pl.BlockSpec(shape, lambda *_: (0,) * len(shape))
    row_spec = lambda width, blk: pl.BlockSpec((nb, width), lambda *_: (0, blk))
    sds = lambda width: jax.ShapeDtypeStruct((nb, width), F32)
    qn, kn, v, beta, eg = pl.pallas_call(
        _gdn_sample_pre_kernel,
        grid=(1,),
        in_specs=[
            row_spec(2 * GDN_KEY, 0), row_spec(GDN_VAL, 1), row_spec(LANE, GDN_BETA_BLK), row_spec(LANE, GDN_BETA_BLK),
            full((3, nb, GDN_CONV_DIM)), full((4, GDN_CONV_DIM)), full((1, LANE)), full((1, LANE)),
        ],
        out_specs=[full((nb, GDN_KEY)), full((nb, GDN_KEY)), full((nb, GDN_VAL)), full((nb, LANE)), full((nb, LANE))],
        out_shape=[sds(GDN_KEY), sds(GDN_KEY), sds(GDN_VAL), sds(LANE), sds(LANE)],
        compiler_params=_cparams("arbitrary"),
        name="gdn_sample_pre",
    )(proj, proj, proj, proj, jnp.swapaxes(conv_prev, 0, 1), conv_w,
      _pad_lanes_at(dt_bias, GDN_A_LANE), _pad_lanes_at(a_log, GDN_A_LANE))
    per_row = lambda width, blk=0: pl.BlockSpec((1, 1, width), lambda b: (b, 0, blk))
    st_spec = pl.BlockSpec((1, GDN_V_HEADS, GDN_D, GDN_D), lambda b: (b, 0, 0, 0))
    r3 = lambda a: a.reshape(nb, 1, -1)
    s_new, o = pl.pallas_call(
        _gdn_sample_state_kernel,
        grid=(nb,),
        in_specs=[st_spec, per_row(GDN_KEY), per_row(GDN_KEY), per_row(GDN_VAL), per_row(GDN_VAL, 2),
                  per_row(LANE), per_row(LANE), pl.BlockSpec((1, GDN_D), lambda b: (0, 0))],
        out_specs=[st_spec, per_row(GDN_VAL)],
        out_shape=[jax.ShapeDtypeStruct(state.shape, F32), jax.ShapeDtypeStruct((nb, 1, GDN_VAL), BF16)],
        compiler_params=_cparams("parallel"),
        name="gdn_sample_state",
    )(state, r3(qn), r3(kn), r3(v), r3(proj), r3(beta), r3(eg), norm_g.reshape(1, -1))
    return o.reshape(nb, GDN_VAL), s_new


def gdn_layer(xp, xs, batch, seq, norm_g_mix, w_in, conv_w, dt_bias, a_log, norm_g, w_out, conv_state, state):
    w_in_b = gdn_weights(w_in)
    w_out_b = w_out.astype(BF16)
    args = (conv_w, dt_bias, a_log, norm_g)
    proj_p = norm_proj(xp, norm_g_mix, w_in_b)
    o_p, st_p = gdn_prompt(proj_p, batch, seq, *args)
    xp = proj_res(o_p, w_out_b, xp)
    raw_p = proj_p.reshape(batch, seq, -1)[:, seq - 3:, :GDN_CONV_DIM]
    proj_s = norm_proj(xs, norm_g_mix, w_in_b)
    o_s, st_s = gdn_sample(proj_s, conv_state, state, *args)
    xs = proj_res(o_s, w_out_b, xs)
    raw_s = jnp.concatenate([conv_state[:, 1:], proj_s[:, None, :GDN_CONV_DIM]], axis=1)
    return xp, xs, raw_p, st_p, raw_s, st_s


NSA_KV = 2
NSA_BLOCK = 64
NSA_TOPN = 16
NSA_WINDOW = 512
NSA_HID = 256
NSA_FORCE = 1e4
NSA_REP = ATT_HEADS // NSA_KV
NSA_KVW = NSA_KV * HEAD_DIM
NSA_QD = ATT_HEADS * HEAD_DIM
NSA_GATE_BLK = (NSA_QD + 6 * NSA_KVW) // LANE
PAGE_SIZE = 128
PAGES_PER_STEP = 128
NSA_KEY_SPAN = 512
CMP_ROWS_PER_STEP = 8
SEL_PER_STEP = 8


def _compress_block_rows(x_refs, pos_ref, w1_ref, w2_ref, o_ref):
    for g, x_ref in enumerate(x_refs):
        nblk = x_ref.shape[0] // NSA_BLOCK
        acc = jnp.zeros((nblk, NSA_HID), F32)
        for t in range(NSA_BLOCK):
            lhs = x_ref[pl.ds(t, nblk, stride=NSA_BLOCK), :] + pos_ref[t:t + 1, :]
            acc = acc + _dot(lhs.astype(BF16), w1_ref[t * HEAD_DIM:(t + 1) * HEAD_DIM, :])
        o_ref[:, g * HEAD_DIM:(g + 1) * HEAD_DIM] = _dot(_silu(acc).astype(BF16), w2_ref[...])


def _compress_rows_kernel(x0_ref, x1_ref, pos_ref, w1_ref, w2_ref, o_ref):
    _compress_block_rows((x0_ref, x1_ref), pos_ref, w1_ref, w2_ref, o_ref)


def compress_rows(proj, col_blk, pos, w1, w2):
    m = proj.shape[0]
    rows = min(m, 8192)
    full = lambda shape: pl.BlockSpec(shape, lambda i: (0,) * len(shape))
    head = lambda g: pl.BlockSpec((rows, HEAD_DIM), lambda i: (i, col_blk * NSA_KV + g))
    return pl.pallas_call(
        _compress_rows_kernel,
        grid=(m // rows,),
        in_specs=[head(0), head(1), full(pos.shape), full(w1.shape), full(w2.shape)],
        out_specs=pl.BlockSpec((rows // NSA_BLOCK, NSA_KVW), lambda i: (i, 0)),
        out_shape=jax.ShapeDtypeStruct((m // NSA_BLOCK, NSA_KVW), F32),
        compiler_params=_cparams("parallel"),
        name="compress_rows",
    )(proj, proj, pos, w1.astype(BF16), w2.astype(BF16))


SUB_BLOCKS = PAGE_SIZE // NSA_BLOCK
BLOCK_ROWS = NSA_BLOCK * NSA_KV


def _compress_paged_kernel(start_ref, pool_ref, pos_ref, w1_ref, w2_ref, o_ref, buf_ref, sem, *, pages):
    i = pl.program_id(0)
    steps = pl.num_programs(0)
    nblk = pages * SUB_BLOCKS

    def block_copy(step, slot, k):
        start = pl.multiple_of(start_ref[step * nblk + k], BLOCK_ROWS)
        return pltpu.make_async_copy(pool_ref.at[pl.ds(start, BLOCK_ROWS), :], buf_ref.at[slot, :, k, :], sem.at[slot])

    def start_step(step, slot):
        def body(k, carry):
            block_copy(step, slot, k).start()
            return carry
        lax.fori_loop(0, nblk, body, 0, unroll=8)

    def wait_step(step, slot):
        def body(k, carry):
            block_copy(step, slot, k).wait()
            return carry
        lax.fori_loop(0, nblk, body, 0, unroll=8)

    slot = lax.rem(i, 2)

    @pl.when(i == 0)
    def _():
        start_step(0, 0)

    @pl.when(i + 1 < steps)
    def _():
        start_step(i + 1, 1 - slot)

    wait_step(i, slot)
    for g in range(NSA_KV):
        acc = jnp.zeros((nblk, NSA_HID), F32)
        for t in range(0, NSA_BLOCK, 2):
            lhs = jnp.concatenate([(buf_ref[slot, NSA_KV * (t + u) + g] + pos_ref[t + u:t + u + 1, :]).astype(BF16)
                                   for u in range(2)], axis=1)
            acc = acc + _dot(lhs, w1_ref[t * HEAD_DIM:(t + 2) * HEAD_DIM, :])
        o_ref[:, g * HEAD_DIM:(g + 1) * HEAD_DIM] = _dot(_silu(acc).astype(BF16), w2_ref[...])


def compress_paged(pool, page_table, pos, w1, w2):
    pt = page_table.reshape(-1)
    pages = min(PAGES_PER_STEP, pt.shape[0])
    steps = pt.shape[0] // pages
    nblk = pages * SUB_BLOCKS
    starts = ((pt[:, None] * SUB_BLOCKS + jnp.arange(SUB_BLOCKS, dtype=pt.dtype)[None, :]) * BLOCK_ROWS).reshape(-1)
    full = lambda shape: pl.BlockSpec(shape, lambda i, pt: (0,) * len(shape))
    return pl.pallas_call(
        functools.partial(_compress_paged_kernel, pages=pages),
        grid_spec=pltpu.PrefetchScalarGridSpec(
            num_scalar_prefetch=1,
            grid=(steps,),
            in_specs=[pl.BlockSpec(memory_space=pl.ANY), full(pos.shape), full(w1.shape), full(w2.shape)],
            out_specs=pl.BlockSpec((nblk, NSA_KVW), lambda i, pt: (i, 0)),
            scratch_shapes=[pltpu.VMEM((2, BLOCK_ROWS, nblk, HEAD_DIM), F32), pltpu.SemaphoreType.DMA((2,))],
        ),
        out_shape=jax.ShapeDtypeStruct((steps * nblk, NSA_KVW), F32),
        compiler_params=_cparams("arbitrary"),
        name="compress_paged",
    )(starts, pool.reshape(-1, HEAD_DIM), pos, w1.astype(BF16), w2.astype(BF16))


def _top_n(score, index, n, axis):
    sel = jnp.zeros(score.shape, F32)
    for _ in range(n):
        m = jnp.max(score, axis=axis, keepdims=True)
        j = jnp.min(jnp.where(score == m, index, 1 << 30), axis=axis, keepdims=True)
        pick = index == j
        sel = jnp.where(pick, 1.0, sel)
        score = jnp.where(pick, -jnp.inf, score)
    return sel


def _nsa_prompt_kernel(q_ref, win_ref, gate_ref, kc_ref, vc_ref, ks_ref, vs_ref, bc_ref, bl_ref, far_ref, o_ref):
    qi = pl.program_id(1)
    seq = ks_ref.shape[0]
    n_cmp = kc_ref.shape[0]
    scale = HEAD_DIM ** -0.5
    gates = _sigmoid(gate_ref[...])
    blk = lax.broadcasted_iota(jnp.int32, (n_cmp, ATT_BLOCK), 0)
    cur = lax.shift_right_logical(qi * ATT_BLOCK + lax.broadcasted_iota(jnp.int32, (n_cmp, ATT_BLOCK), 1), 6)
    valid = blk <= cur
    forced = valid & ((blk == 0) | (blk >= cur - 1))
    qh_all, o_cmp_all, sel_all = [], [], []
    for g in range(NSA_KV):
        kcg = kc_ref[:, g * HEAD_DIM:(g + 1) * HEAD_DIM].astype(BF16)
        vcg = vc_ref[:, g * HEAD_DIM:(g + 1) * HEAD_DIM].astype(BF16)
        qh = [(q_ref[:, (g * NSA_REP + r) * HEAD_DIM:(g * NSA_REP + r + 1) * HEAD_DIM] * scale).astype(BF16)
              for r in range(NSA_REP)]
        bias_c = bc_ref[g, 0]
        ok_c = bias_c > 0.5 * NEG
        lc = _dot_nt(kcg, jnp.concatenate(qh, axis=0)) + bias_c
        p = jnp.where(ok_c, jnp.exp(lc - jnp.max(lc, axis=0, keepdims=True)), 0.0)
        p = p / jnp.maximum(jnp.sum(p, axis=0, keepdims=True), 1e-30)
        p_slc = functools.reduce(lambda a, b: a + b, [p[:, r * ATT_BLOCK:(r + 1) * ATT_BLOCK] for r in range(NSA_REP)])
        score = jnp.where(forced, NSA_FORCE, jnp.where(valid, p_slc, -1.0))
        qh_all.append(qh)
        o_cmp_all.append(_dot_tn(p.astype(BF16), vcg))
        sel_all.append(jnp.where(valid, _top_n(score, blk, NSA_TOPN, 0), 0.0).astype(BF16))

    def selected_and_combine(n_keys):
        er = lax.broadcasted_iota(jnp.int32, (n_cmp, n_keys), 0)
        ec = lax.broadcasted_iota(jnp.int32, (n_cmp, n_keys), 1)
        expand = jnp.where(lax.shift_right_logical(ec, 6) == er, 1.0, 0.0).astype(BF16)
        for g in range(NSA_KV):
            sel_bias = jnp.where(_dot_tn(sel_all[g], expand) > 0.5, 0.0, NEG)
            ksg = ks_ref[0:n_keys, g * HEAD_DIM:(g + 1) * HEAD_DIM].astype(BF16)
            vsg = vs_ref[0:n_keys, g * HEAD_DIM:(g + 1) * HEAD_DIM].astype(BF16)
            for r in range(NSA_REP):
                h = g * NSA_REP + r
                pieces = []
                for c in range(n_keys // ATT_BLOCK):
                    delta = qi - c
                    far = jnp.where(delta >= 2, far_ref[h:h + 1, 0:1], NEG)
                    pieces.append(jnp.where(delta == 0, bl_ref[h, :, ATT_BLOCK:2 * ATT_BLOCK],
                                            jnp.where(delta == 1, bl_ref[h, :, 0:ATT_BLOCK], far)))
                ls = _dot_nt(qh_all[g][r], ksg) + jnp.concatenate(pieces, axis=1) + sel_bias
                ps = jnp.exp(ls - jnp.max(ls, axis=-1, keepdims=True))
                den = jnp.sum(ps, axis=-1, keepdims=True)
                o_sel = _dot(ps.astype(BF16), vsg) / den
                out = (gates[:, 3 * h:3 * h + 1] * o_cmp_all[g][r * ATT_BLOCK:(r + 1) * ATT_BLOCK]
                       + gates[:, 3 * h + 1:3 * h + 2] * o_sel
                       + gates[:, 3 * h + 2:3 * h + 3] * win_ref[:, h * HEAD_DIM:(h + 1) * HEAD_DIM])
                o_ref[:, h * HEAD_DIM:(h + 1) * HEAD_DIM] = out.astype(BF16)

    tiles_per_span = NSA_KEY_SPAN // ATT_BLOCK
    for c in range(seq // NSA_KEY_SPAN):
        pl.when(qi // tiles_per_span == c)(functools.partial(selected_and_combine, (c + 1) * NSA_KEY_SPAN))


def nsa_prompt_attention(proj, o_win, kc_blk, vc_blk, table, batch, seq):
    nq = seq // ATT_BLOCK
    n_cmp = seq // NSA_BLOCK
    assert n_cmp >= NSA_TOPN and seq % NSA_KEY_SPAN == 0
    dist = jnp.arange(seq)[:, None] - (jnp.arange(n_cmp) * NSA_BLOCK + NSA_BLOCK - 1)[None, :]
    bias_c = jnp.where(dist >= 0, head_bias(table, dist), NEG)
    bias_c = bias_c.reshape(NSA_KV, NSA_REP, nq, ATT_BLOCK, n_cmp).transpose(0, 2, 4, 1, 3)
    bias_c = bias_c.reshape(NSA_KV, nq, n_cmp, NSA_REP * ATT_BLOCK)
    bias_l = local_bias(table, 1, 1 << 30, ATT_HEADS).reshape(ATT_HEADS, ATT_BLOCK, 2 * ATT_BLOCK)
    far = jnp.broadcast_to(table[REL_BUCKETS - 1][:, None].astype(F32), (ATT_HEADS, LANE))
    full = lambda shape: pl.BlockSpec(shape, lambda b, i: (0,) * len(shape))
    return pl.pallas_call(
        _nsa_prompt_kernel,
        grid=(batch, nq),
        in_specs=[
            pl.BlockSpec((ATT_BLOCK, NSA_QD), lambda b, i: (b * nq + i, 0)),
            pl.BlockSpec((ATT_BLOCK, NSA_QD), lambda b, i: (b * nq + i, 0)),
            pl.BlockSpec((ATT_BLOCK, LANE), lambda b, i: (b * nq + i, NSA_GATE_BLK)),
            pl.BlockSpec((n_cmp, NSA_KVW), lambda b, i: (b, 0)),
            pl.BlockSpec((n_cmp, NSA_KVW), lambda b, i: (b, 0)),
            pl.BlockSpec((seq, NSA_KVW), lambda b, i: (b, NSA_QD // NSA_KVW + 2)),
            pl.BlockSpec((seq, NSA_KVW), lambda b, i: (b, NSA_QD // NSA_KVW + 3)),
            pl.BlockSpec((NSA_KV, 1, n_cmp, NSA_REP * ATT_BLOCK), lambda b, i: (0, i, 0, 0)),
            full(bias_l.shape), full(far.shape),
        ],
        out_specs=pl.BlockSpec((ATT_BLOCK, NSA_QD), lambda b, i: (b * nq + i, 0)),
        out_shape=jax.ShapeDtypeStruct((batch * seq, NSA_QD), BF16),
        compiler_params=_cparams("parallel", "arbitrary"),
        name="nsa_prompt_attention",
    )(proj, o_win, proj, kc_blk, vc_blk, proj, proj, bias_c, bias_l, far)


def _nsa_sample_cmp_kernel(q_ref, kc_ref, vc_ref, bias_ref, o_ref, idx_ref, *, cur, n_sel):
    rows_per_step, n_cmp = kc_ref.shape[0], kc_ref.shape[1]
    n_rows = rows_per_step * NSA_KV
    width = _round_up(n_sel, LANE)
    scale = HEAD_DIM ** -0.5
    lane = lax.broadcasted_iota(jnp.int32, (n_rows, width), 1)
    lane_out = lax.broadcasted_iota(jnp.int32, (n_rows, LANE), 1)
    valid = lane <= cur
    forced = valid & ((lane == 0) | (lane >= cur - 1))
    p_slc = []
    for bb in range(rows_per_step):
        for g in range(NSA_KV):
            qg = jnp.concatenate([q_ref[bb, :, (g * NSA_REP + r) * HEAD_DIM:(g * NSA_REP + r + 1) * HEAD_DIM]
                                  for r in range(NSA_REP)], axis=0) * scale
            bias = bias_ref[g * NSA_REP:(g + 1) * NSA_REP, :]
            ok = bias > 0.5 * NEG
            lc = _dot_nt(qg.astype(BF16), kc_ref[bb, :, g * HEAD_DIM:(g + 1) * HEAD_DIM].astype(BF16)) + bias
            p = jnp.where(ok, jnp.exp(lc - jnp.max(lc, axis=-1, keepdims=True)), 0.0)
            p = p / jnp.maximum(jnp.sum(p, axis=-1, keepdims=True), 1e-30)
            o_cmp = _dot(p.astype(BF16), vc_ref[bb, :, g * HEAD_DIM:(g + 1) * HEAD_DIM].astype(BF16))
            for r in range(NSA_REP):
                h = g * NSA_REP + r
                o_ref[bb, :, h * HEAD_DIM:(h + 1) * HEAD_DIM] = o_cmp[r:r + 1, :]
            p_slc.append(jnp.sum(p, axis=0, keepdims=True))
    score = jnp.concatenate([jnp.concatenate(p_slc, axis=0), jnp.zeros((n_rows, width - n_cmp), F32)], axis=1)
    score = jnp.where(forced, NSA_FORCE, jnp.where(valid, score, -1.0))
    score = jnp.where(lane < n_sel, score, -jnp.inf)
    idx_rows = jnp.zeros((n_rows, LANE), jnp.int32)
    for it in range(NSA_TOPN):
        m = jnp.max(score, axis=-1, keepdims=True)
        j = jnp.min(jnp.where(score == m, lane, 1 << 30), axis=-1, keepdims=True)
        idx_rows = jnp.where(lane_out == it, j, idx_rows)
        score = jnp.where(lane == j, -jnp.inf, score)
    for bb in range(rows_per_step):
        for g in range(NSA_KV):
            idx_ref[bb, :, g * LANE:(g + 1) * LANE] = idx_rows[bb * NSA_KV + g:bb * NSA_KV + g + 1, :]


def _nsa_sample_sel_kernel(idx_ref, pt_ref, q_ref, kn_ref, vn_ref, *refs, cur, n_past_blk):
    blocks = refs[:SEL_PER_STEP * NSA_KV * 3]
    o_ref, m_ref, l_ref, acc_ref = refs[SEL_PER_STEP * NSA_KV * 3:]
    b = pl.program_id(0)
    n = pl.program_id(1)
    scale = HEAD_DIM ** -0.5

    @pl.when(n == 0)
    def _():
        m_ref[...] = jnp.full(m_ref.shape, NEG, F32)
        l_ref[...] = jnp.zeros_like(l_ref)
        acc_ref[...] = jnp.zeros_like(acc_ref)

    row = lax.broadcasted_iota(jnp.int32, (BLOCK_ROWS, HEAD_DIM), 0)
    key_head = lax.rem(lax.broadcasted_iota(jnp.int32, (1, BLOCK_ROWS), 1), NSA_KV)
    for g in range(NSA_KV):
        qg = (jnp.concatenate([q_ref[0, :, (g * NSA_REP + r) * HEAD_DIM:(g * NSA_REP + r + 1) * HEAD_DIM]
                               for r in range(NSA_REP)], axis=0) * scale).astype(BF16)
        knew = jnp.where(row == g, kn_ref[0, :, g * HEAD_DIM:(g + 1) * HEAD_DIM], 0.0)
        vnew = jnp.where(row == g, vn_ref[0, :, g * HEAD_DIM:(g + 1) * HEAD_DIM], 0.0)
        rows = slice(g * NSA_REP, (g + 1) * NSA_REP)
        logits, vals, oks = [], [], []
        for j in range(SEL_PER_STEP):
            k_ref, v_ref, bias_ref = blocks[(j * NSA_KV + g) * 3:(j * NSA_KV + g) * 3 + 3]
            i = idx_ref[(b * NSA_KV + g) * NSA_TOPN + n * SEL_PER_STEP + j]
            is_new = i >= n_past_blk
            kblk = jnp.where(is_new, knew, k_ref[...]).astype(BF16)
            vals.append(jnp.where(is_new, vnew, v_ref[...]).astype(BF16))
            bias = bias_ref[0, rows, :]
            ok = (bias > 0.5 * NEG) & (key_head == g) & (i <= cur)
            oks.append(ok)
            logits.append(jnp.where(ok, _dot_nt(qg, kblk) + bias, NEG))
        m_old = m_ref[rows, :]
        m_new = functools.reduce(jnp.maximum, [m_old] + [jnp.max(ls, axis=-1, keepdims=True) for ls in logits])
        alpha = jnp.exp(m_old - m_new)
        l_new = alpha * l_ref[rows, :]
        acc = alpha * acc_ref[rows, :]
        for ls, ok, vblk in zip(logits, oks, vals):
            ps = jnp.where(ok, jnp.exp(ls - m_new[:, 0:1]), 0.0)
            l_new = l_new + jnp.sum(ps, axis=-1, keepdims=True)
            acc = acc + _dot(ps.astype(BF16), vblk)
        l_ref[rows, :] = l_new
        acc_ref[rows, :] = acc
        m_ref[rows, :] = m_new

    @pl.when(n == pl.num_programs(1) - 1)
    def _():
        out = acc_ref[...] / jnp.maximum(l_ref[...], 1e-30)
        for h in range(ATT_HEADS):
            o_ref[0, :, h * HEAD_DIM:(h + 1) * HEAD_DIM] = out[h:h + 1, :]


def _nsa_combine_kernel(gate_ref, c_ref, s_ref, w_ref, o_ref):
    gates = _sigmoid(gate_ref[...])
    for h in range(ATT_HEADS):
        cols = slice(h * HEAD_DIM, (h + 1) * HEAD_DIM)
        out = (gates[:, 3 * h:3 * h + 1] * c_ref[:, cols] + gates[:, 3 * h + 1:3 * h + 2] * s_ref[:, cols]
               + gates[:, 3 * h + 2:3 * h + 3] * w_ref[:, cols])
        o_ref[:, cols] = out.astype(BF16)


def nsa_sample_attention(proj, kc_blk, vc_blk, sel_k_pool, sel_v_pool, win_k, win_v, page_table, table, past):
    nb = proj.shape[0]
    n_pages = page_table.shape[1]
    n_cmp = kc_blk.shape[0] // nb
    sub = PAGE_SIZE // NSA_BLOCK
    n_past_blk = n_pages * sub
    cur = past // NSA_BLOCK
    n_sel = -(-(past + 1) // NSA_BLOCK)
    proj3 = proj.reshape(nb, 1, -1)
    full = lambda shape: pl.BlockSpec(shape, lambda *_: (0,) * len(shape))
    dist_c = past - (jnp.arange(n_cmp) * NSA_BLOCK + NSA_BLOCK - 1)
    bias_c = jnp.where(dist_c >= 0, head_bias(table, dist_c), NEG)
    rs = math.gcd(nb, CMP_ROWS_PER_STEP)
    o_cmp, idx = pl.pallas_call(
        functools.partial(_nsa_sample_cmp_kernel, cur=cur, n_sel=n_sel),
        grid=(nb // rs,),
        in_specs=[
            pl.BlockSpec((rs, 1, NSA_QD), lambda b: (b, 0, 0)),
            pl.BlockSpec((rs, n_cmp, NSA_KVW), lambda b: (b, 0, 0)),
            pl.BlockSpec((rs, n_cmp, NSA_KVW), lambda b: (b, 0, 0)),
            full(bias_c.shape),
        ],
        out_specs=[pl.BlockSpec((rs, 1, NSA_QD), lambda b: (b, 0, 0)), pl.BlockSpec((rs, 1, NSA_KV * LANE), lambda b: (b, 0, 0))],
        out_shape=[jax.ShapeDtypeStruct((nb, 1, NSA_QD), F32), jax.ShapeDtypeStruct((nb, 1, NSA_KV * LANE), jnp.int32)],
        compiler_params=_cparams("parallel"),
        name="nsa_sample_cmp",
    )(proj3, kc_blk.reshape(nb, n_cmp, NSA_KVW), vc_blk.reshape(nb, n_cmp, NSA_KVW), bias_c)
    idx_flat = idx.reshape(nb, NSA_KV, LANE)[:, :, :NSA_TOPN].reshape(-1)

    dist_s = past - jnp.arange(n_sel * NSA_BLOCK)
    bias_s = jnp.where(dist_s >= 0, head_bias(table, dist_s), NEG).reshape(ATT_HEADS, n_sel, NSA_BLOCK)
    bias_s = jnp.repeat(jnp.swapaxes(bias_s, 0, 1), NSA_KV, axis=2)

    def chosen(b, n, j, g, idx_ref):
        return idx_ref[(b * NSA_KV + g) * NSA_TOPN + n * SEL_PER_STEP + j]

    def phys(b, n, j, g, idx_ref, pt_ref):
        i = jnp.minimum(chosen(b, n, j, g, idx_ref), n_past_blk - 1)
        return pt_ref[b, i // sub] * sub + i % sub

    pool_spec = lambda j, g: pl.BlockSpec((BLOCK_ROWS, HEAD_DIM),
                                          lambda b, n, idx_ref, pt_ref: (phys(b, n, j, g, idx_ref, pt_ref), 0))
    bias_spec = lambda j, g: pl.BlockSpec((1, ATT_HEADS, BLOCK_ROWS),
                                          lambda b, n, idx_ref, pt_ref: (chosen(b, n, j, g, idx_ref), 0, 0))
    row_spec = lambda width, blk: pl.BlockSpec((1, 1, width), lambda b, n, idx_ref, pt_ref: (b, 0, blk))
    pool_k = sel_k_pool.reshape(-1, HEAD_DIM)
    pool_v = sel_v_pool.reshape(-1, HEAD_DIM)
    block_specs, block_args = [], []
    for j in range(SEL_PER_STEP):
        for g in range(NSA_KV):
            block_specs += [pool_spec(j, g), pool_spec(j, g), bias_spec(j, g)]
            block_args += [pool_k, pool_v, bias_s]
    o_sel = pl.pallas_call(
        functools.partial(_nsa_sample_sel_kernel, cur=cur, n_past_blk=n_past_blk),
        grid_spec=pltpu.PrefetchScalarGridSpec(
            num_scalar_prefetch=2,
            grid=(nb, NSA_TOPN // SEL_PER_STEP),
            in_specs=[row_spec(NSA_QD, 0), row_spec(NSA_KVW, NSA_QD // NSA_KVW + 2),
                      row_spec(NSA_KVW, NSA_QD // NSA_KVW + 3)] + block_specs,
            out_specs=row_spec(NSA_QD, 0),
            scratch_shapes=[pltpu.VMEM((ATT_HEADS, LANE), F32), pltpu.VMEM((ATT_HEADS, LANE), F32),
                            pltpu.VMEM((ATT_HEADS, HEAD_DIM), F32)],
        ),
        out_shape=jax.ShapeDtypeStruct((nb, 1, NSA_QD), F32),
        compiler_params=_cparams("parallel", "arbitrary"),
        name="nsa_sample_sel",
    )(idx_flat, page_table, proj3, proj3, proj3, *block_args)

    o_win = decode_attention(proj, 0, NSA_QD // NSA_KVW + 4, NSA_QD // NSA_KVW + 5, win_k, win_v, NSA_KV, table, None, F32)
    row2 = lambda width, blk: pl.BlockSpec((nb, width), lambda i: (0, blk))
    return pl.pallas_call(
        _nsa_combine_kernel,
        grid=(1,),
        in_specs=[row2(LANE, NSA_GATE_BLK), row2(NSA_QD, 0), row2(NSA_QD, 0), row2(NSA_QD, 0)],
        out_specs=row2(NSA_QD, 0),
        out_shape=jax.ShapeDtypeStruct((nb, NSA_QD), BF16),
        compiler_params=_cparams("arbitrary"),
        name="nsa_combine",
    )(proj, o_cmp.reshape(nb, NSA_QD), o_sel.reshape(nb, NSA_QD), o_win)


def nsa_weights(w_in):
    return proj_weights(w_in)


def nsa_layer(xp, xs, batch, seq, past, norm_g_mix, table, w_in, pos_k, pos_v, w1_k, w2_k, w1_v, w2_v, w_o,
              cmp_k_pool, cmp_v_pool, sel_k_pool, sel_v_pool, win_k, win_v, page_table):
    w_in_b = nsa_weights(w_in)
    w_o_b = w_o.astype(BF16)
    kv0 = NSA_QD // NSA_KVW
    nb = xs.shape[0]

    proj_p = norm_proj(xp, norm_g_mix, w_in_b)
    kc_blk = compress_rows(proj_p, kv0, pos_k, w1_k, w2_k)
    vc_blk = compress_rows(proj_p, kv0 + 1, pos_v, w1_v, w2_v)
    n_prev = -(-NSA_WINDOW // ATT_BLOCK)
    bias_w = local_bias(table, n_prev, NSA_WINDOW, NSA_KV)
    o_win = banded_attention(proj_p, batch, seq, 0, kv0 + 4, kv0 + 5, NSA_KV, n_prev, bias_w, None, F32)
    o_p = nsa_prompt_attention(proj_p, o_win, kc_blk, vc_blk, table, batch, seq)
    xp = proj_res(o_p, w_o_b, xp)
    kvs_p = proj_p.reshape(batch, seq, -1)[:, :, NSA_QD:NSA_QD + 6 * NSA_KVW].reshape(batch, seq, 6, NSA_KV, HEAD_DIM)
    buf = min(NSA_WINDOW, seq)
    outs_p = [kvs_p[:, :, j] for j in range(4)] + [kvs_p[:, seq - buf:, 4], kvs_p[:, seq - buf:, 5]]

    proj_s = norm_proj(xs, norm_g_mix, w_in_b)
    kc_s = compress_paged(cmp_k_pool, page_table, pos_k, w1_k, w2_k)
    vc_s = compress_paged(cmp_v_pool, page_table, pos_v, w1_v, w2_v)
    o_s = nsa_sample_attention(proj_s, kc_s, vc_s, sel_k_pool, sel_v_pool, win_k, win_v, page_table, table, past)
    xs = proj_res(o_s, w_o_b, xs)
    kvs_s = proj_s[:, NSA_QD:NSA_QD + 6 * NSA_KVW].reshape(nb, 1, 6, NSA_KV, HEAD_DIM)
    outs_s = [kvs_s[:, :, j] for j in range(4)]
    outs_s += [jnp.concatenate([win_k[:, 1:], kvs_s[:, :, 4]], axis=1), jnp.concatenate([win_v[:, 1:], kvs_s[:, :, 5]], axis=1)]
    return xp, xs, outs_p, outs_s


def kernel(x_prompt, x_sample, state_ssd_conv, state_ssd, cache_swa_k, cache_swa_v, state_gdn_conv, state_gdn,
           cache_nsa_cmp_k, cache_nsa_cmp_v, cache_nsa_sel_k, cache_nsa_sel_v, cache_nsa_win_k, cache_nsa_win_v,
           page_table, rel_table, norm_ffn1, norm_mix, norm_ffn2, norm_final,
           ffn1_gate, ffn1_up, ffn1_down, ffn2_gate, ffn2_up, ffn2_down,
           ssd_w_in, ssd_conv_w, ssd_conv_b, ssd_dt_bias, ssd_a_log, ssd_d, ssd_norm, ssd_w_out,
           swa_w_qkv, swa_sink, swa_w_o,
           gdn_w_in, gdn_conv_w, gdn_dt_bias, gdn_a_log, gdn_norm, gdn_w_out,
           nsa_w_in, nsa_pos_k, nsa_pos_v, nsa_cmp_w1_k, nsa_cmp_w2_k, nsa_cmp_w1_v, nsa_cmp_w2_v, nsa_w_o):
    batch, seq, d = x_prompt.shape
    nb = x_sample.shape[0]
    depth = norm_mix.shape[0]
    past = page_table.shape[1] * PAGE_SIZE
    xp = x_prompt.reshape(batch * seq, d)
    xs = x_sample.reshape(nb, d)
    outs_p = {}
    outs_s = {}

    def put(store, name, value):
        store.setdefault(name, []).append(value)

    w1 = ffn_weights(ffn1_gate, ffn1_up, ffn1_down)
    w2 = ffn_weights(ffn2_gate, ffn2_up, ffn2_down)
    for i in range(depth):
        kind, li = i % 4, i // 4
        xp = ffn(xp, norm_ffn1[i], *w1, i)
        xs = ffn(xs, norm_ffn1[i], *w1, i)
        if kind == 0:
            xp, xs, cp, sp, cs, ss = ssd_layer(
                xp, xs, batch, seq, norm_mix[i], ssd_w_in[li], ssd_conv_w[li], ssd_conv_b[li], ssd_dt_bias[li],
                ssd_a_log[li], ssd_d[li], ssd_norm[li], ssd_w_out[li], state_ssd_conv[li], state_ssd[li])
            put(outs_p, "ssd_conv", cp), put(outs_p, "ssd_state", sp)
            put(outs_s, "ssd_conv", cs), put(outs_s, "ssd_state", ss)
        elif kind == 1:
            xp, xs, kp, vp, kq, vq = swa_layer(xp, xs, batch, seq, norm_mix[i], rel_table, swa_w_qkv[li], swa_sink[li],
                                               swa_w_o[li], cache_swa_k[li], cache_swa_v[li])
            put(outs_p, "swa_k", kp), put(outs_p, "swa_v", vp)
            put(outs_s, "swa_k", kq), put(outs_s, "swa_v", vq)
        elif kind == 2:
            xp, xs, cp, sp, cs, ss = gdn_layer(
                xp, xs, batch, seq, norm_mix[i], gdn_w_in[li], gdn_conv_w[li], gdn_dt_bias[li], gdn_a_log[li],
                gdn_norm[li], gdn_w_out[li], state_gdn_conv[li], state_gdn[li])
            put(outs_p, "gdn_conv", cp), put(outs_p, "gdn_state", sp)
            put(outs_s, "gdn_conv", cs), put(outs_s, "gdn_state", ss)
        else:
            xp, xs, op, os_ = nsa_layer(
                xp, xs, batch, seq, past, norm_mix[i], rel_table, nsa_w_in[li], nsa_pos_k[li], nsa_pos_v[li],
                nsa_cmp_w1_k[li], nsa_cmp_w2_k[li], nsa_cmp_w1_v[li], nsa_cmp_w2_v[li], nsa_w_o[li],
                cache_nsa_cmp_k[li], cache_nsa_cmp_v[li], cache_nsa_sel_k[li], cache_nsa_sel_v[li],
                cache_nsa_win_k[li], cache_nsa_win_v[li], page_table)
            for j in range(6):
                put(outs_p, f"nsa{j}", op[j]), put(outs_s, f"nsa{j}", os_[j])
        xp = ffn(xp, norm_ffn2[i], *w2, i)
        xs = ffn(xs, norm_ffn2[i], *w2, i)

    y_prompt = final_norm(xp, norm_final).reshape(batch, seq, d)
    y_sample = final_norm(xs, norm_final).reshape(nb, 1, d)
    order = ["ssd_conv", "ssd_state", "swa_k", "swa_v", "gdn_conv", "gdn_state"] + [f"nsa{j}" for j in range(6)]
    states_p = tuple(jnp.stack(outs_p[name]) for name in order)
    states_s = tuple(jnp.stack(outs_s[name]) for name in order)
    return (y_prompt, y_sample) + states_p + states_s
```
